```python
import jax, jax.numpy as jnp
from jax import lax
import numpy as np

D_MODEL = 1024
BATCH = 16
SEQ = 256
DEPTH = 2
DEC_BATCH = 8
DEC_SEQ = 4096
PAST_LEN = 256

GRID_W = 64
EPS = 1e-6
A_WIDTH = D_MODEL // 2
A_HEADS = 4
A_DV = A_WIDTH // A_HEADS
A_KWIDTH = A_WIDTH // 2
A_DK = A_KWIDTH // A_HEADS
A_RANK = 16
A_TAU = 16.0
A_CHUNK = 64
B_WIDTH = D_MODEL // 4
B_KSIZE = 31
C_WIDTH = D_MODEL // 4
C_HEADS = 4
C_HEAD_DIM = C_WIDTH // C_HEADS
C_CHUNK = 128
MIX_WIDTH = A_WIDTH + B_WIDTH + C_WIDTH
D_FF = 4 * D_MODEL
N_MOD = 6
IN_SPLITS = (
    A_KWIDTH,
    2 * A_KWIDTH,
    2 * A_KWIDTH + A_WIDTH,
    2 * A_KWIDTH + 2 * A_WIDTH,
    2 * A_KWIDTH + 2 * A_WIDTH + 2 * A_RANK,
    2 * A_KWIDTH + 2 * A_WIDTH + 2 * A_RANK + 2 * B_WIDTH,
)
IN_WIDTH = 2 * A_KWIDTH + 2 * A_WIDTH + 2 * A_RANK + 2 * B_WIDTH + 2 * C_WIDTH

kernel_name = "hybrid_gla_conformer_gmlp_diffusion_step"


def rmsnorm(x, g):
    xf = x.astype(jnp.float32)
    y = xf * lax.rsqrt(jnp.mean(xf * xf, axis=-1, keepdims=True) + EPS)
    return (y * g.astype(jnp.float32)).astype(x.dtype)


def layernorm(x, g, b):
    xf = x.astype(jnp.float32)
    mu = jnp.mean(xf, axis=-1, keepdims=True)
    xc = xf - mu
    var = jnp.mean(xc * xc, axis=-1, keepdims=True)
    y = xc * lax.rsqrt(var + EPS) * g.astype(jnp.float32) + b.astype(jnp.float32)
    return y.astype(x.dtype)


def adaln(cond, w_mod, b_mod):
    mod = jax.nn.silu(cond) @ w_mod + b_mod
    mod = mod.reshape(mod.shape[:-1] + (N_MOD, D_MODEL))
    return tuple(mod[..., i, None, :] for i in range(N_MOD))


def gla_scan(q, k, v, log_a, s0):
    bsz, t = q.shape[:2]
    n = t // A_CHUNK

    def chunks(a):
        return a.reshape((bsz, n, A_CHUNK) + a.shape[2:])

    q, k, v, log_a = chunks(q), chunks(k), chunks(v), chunks(log_a)
    b = jnp.cumsum(log_a, axis=2)
    b_last = b[:, :, -1:]
    q_dec = q * jnp.exp(b)
    k_dec = k * jnp.exp(-b)
    k_end = k * jnp.exp(b_last - b)
    mask = jnp.tril(jnp.ones((A_CHUNK, A_CHUNK), dtype=bool))
    att = jnp.einsum("bnthd,bnshd->bnhts", q_dec, k_dec)
    att = jnp.where(mask, att, 0.0)
    o_intra = jnp.einsum("bnhts,bnshv->bnthv", att, v)
    ds = jnp.einsum("bnshd,bnshv->bnhdv", k_end, v)
    decay = jnp.exp(b_last[:, :, 0])

    def step(s, inp):
        d, dsn = inp
        return d[..., None] * s + dsn, s

    s_final, s_in = lax.scan(step, s0, (jnp.moveaxis(decay, 1, 0), jnp.moveaxis(ds, 1, 0)))
    s_in = jnp.moveaxis(s_in, 0, 1)
    o_inter = jnp.einsum("bnthd,bnhdv->bnthv", q_dec, s_in)
    o = (o_intra + o_inter).reshape(bsz, t, A_HEADS, A_DV)
    return o, s_final


def gla_group(q, k, v, g, lr, w_a_gate, b_a_gate, a_norm_g, s0_fwd, s0_bwd):
    bsz, t = q.shape[:2]
    f32 = jnp.float32
    qh = q.astype(f32).reshape(bsz, t, A_HEADS, A_DK) * (A_DK ** -0.5)
    kh = k.astype(f32).reshape(bsz, t, A_HEADS, A_DK)
    vh = v.astype(f32).reshape(bsz, t, A_HEADS, A_DV)
    logits = jnp.einsum("btzr,zrk->btzk", lr.reshape(bsz, t, 2, A_RANK), w_a_gate) + b_a_gate
    log_a = (jax.nn.log_sigmoid(logits.astype(f32)) / A_TAU).reshape(bsz, t, 2, A_HEADS, A_DK)
    o_f, s_f = gla_scan(qh, kh, vh, log_a[:, :, 0], s0_fwd)
    o_b, s_b = gla_scan(qh[:, ::-1], kh[:, ::-1], vh[:, ::-1], log_a[:, ::-1, 1], s0_bwd)
    o = rmsnorm(o_f + o_b[:, ::-1], a_norm_g).astype(g.dtype)
    o = o.reshape(bsz, t, A_WIDTH) * jax.nn.silu(g)
    return o, s_f, s_b


def conv_group(glu, w_dw, ln_g, ln_b, grid):
    bsz, t = glu.shape[:2]
    a, b = jnp.split(glu, 2, axis=-1)
    xc = a * jax.nn.sigmoid(b)
    if grid:
        rows = t // GRID_W
        xc = xc.reshape(bsz * rows, GRID_W, B_WIDTH)
    y = lax.conv_general_dilated(
        xc, w_dw, window_strides=(1,), padding=[(B_KSIZE // 2, B_KSIZE // 2)],
        dimension_numbers=("NWC", "WIO", "NWC"), feature_group_count=B_WIDTH)
    y = y.reshape(bsz, t, B_WIDTH)
    return jax.nn.silu(layernorm(y, ln_g, ln_b))


def gmlp_group(uv, ln_g, ln_b, w_s, b_s):
    bsz, t = uv.shape[:2]
    u, v = jnp.split(jax.nn.gelu(uv), 2, axis=-1)
    v = layernorm(v, ln_g, ln_b).reshape(bsz, t // C_CHUNK, C_CHUNK, C_HEADS, C_HEAD_DIM)
    sv = jnp.einsum("hpq,bnqhd->bnphd", w_s, v) + b_s.T[:, :, None]
    return u * sv.reshape(bsz, t, C_WIDTH)


def layer(x, mod, lp, s0_fwd, s0_bwd, grid):
    (norm1_g, w_in, w_a_gate, b_a_gate, a_norm_g, w_dw, b_ln_g, b_ln_b,
     c_ln_g, c_ln_b, w_s, b_s, w_out, norm2_g, w_ff1, w_ff2) = lp
    shift1, scale1, gate1, shift2, scale2, gate2 = mod
    h = rmsnorm(x, norm1_g) * (1 + scale1) + shift1
    z = h @ w_in
    q, k, v, g, lr, glu, uv = jnp.split(z, IN_SPLITS, axis=-1)
    o_a, s_f, s_b = gla_group(q, k, v, g, lr, w_a_gate, b_a_gate, a_norm_g, s0_fwd, s0_bwd)
    o_b = conv_group(glu, w_dw, b_ln_g, b_ln_b, grid)
    o_c = gmlp_group(uv, c_ln_g, c_ln_b, w_s, b_s)
    mix = jnp.concatenate([o_a, o_b.astype(o_a.dtype), o_c.astype(o_a.dtype)], axis=-1) @ w_out
    x = x + gate1 * mix
    h = rmsnorm(x, norm2_g) * (1 + scale2) + shift2
    x = x + gate2 * (jnp.square(jax.nn.relu(h @ w_ff1)) @ w_ff2)
    return x, s_f, s_b


def setup_inputs(seed: int = 0) -> dict:
    key = jax.random.key(seed)
    ks = jax.random.split(key, 32)
    f32 = jnp.float32
    L = DEPTH

    def nrm(k, shape, scale):
        return jax.random.normal(k, shape, f32) * scale

    return {
        "x_prompt": nrm(ks[0], (BATCH, SEQ, D_MODEL), 1.0),
        "x_sample": nrm(ks[1], (DEC_BATCH, DEC_SEQ, D_MODEL), 1.0),
        "c": nrm(ks[2], (DEC_BATCH, D_MODEL), 1.0),
        "state_gla": nrm(ks[3], (DEC_BATCH, DEPTH, 2, A_HEADS, A_DK, A_DV), 1.0),
        "c_ctx": nrm(ks[4], (D_MODEL,), 1.0),
        "w_mod": nrm(ks[5], (L, D_MODEL, N_MOD * D_MODEL), 0.5 * D_MODEL ** -0.5),
        "b_mod": nrm(ks[6], (L, N_MOD * D_MODEL), 0.05),
        "norm1_g": 1.0 + nrm(ks[7], (L, D_MODEL), 0.05),
        "w_in": nrm(ks[8], (L, D_MODEL, IN_WIDTH), D_MODEL ** -0.5),
        "w_a_gate": nrm(ks[9], (L, 2, A_RANK, A_KWIDTH), A_RANK ** -0.5),
        "b_a_gate": nrm(ks[10], (L, 2, A_KWIDTH), 0.5),
        "a_norm_g": 1.0 + nrm(ks[11], (L, A_DV), 0.05),
        "w_dw": nrm(ks[12], (L, B_KSIZE, 1, B_WIDTH), B_KSIZE ** -0.5),
        "b_ln_g": 1.0 + nrm(ks[13], (L, B_WIDTH), 0.05),
        "b_ln_b": nrm(ks[14], (L, B_WIDTH), 0.05),
        "c_ln_g": 1.0 + nrm(ks[15], (L, C_WIDTH), 0.05),
        "c_ln_b": nrm(ks[16], (L, C_WIDTH), 0.05),
        "w_s": nrm(ks[17], (L, C_HEADS, C_CHUNK, C_CHUNK), C_CHUNK ** -0.5),
        "b_s": 1.0 + nrm(ks[18], (L, C_HEADS, C_CHUNK), 0.1),
        "w_out": nrm(ks[19], (L, MIX_WIDTH, D_MODEL), MIX_WIDTH ** -0.5),
        "norm2_g": 1.0 + nrm(ks[20], (L, D_MODEL), 0.05),
        "w_ff1": nrm(ks[21], (L, D_MODEL, D_FF), D_MODEL ** -0.5),
        "w_ff2": nrm(ks[22], (L, D_FF, D_MODEL), D_FF ** -0.5),
        "final_g": 1.0 + nrm(ks[23], (D_MODEL,), 0.05),
    }


def reference(x_prompt, x_sample, c, state_gla, c_ctx, w_mod, b_mod, norm1_g, w_in,
              w_a_gate, b_a_gate, a_norm_g, w_dw, b_ln_g, b_ln_b, c_ln_g, c_ln_b,
              w_s, b_s, w_out, norm2_g, w_ff1, w_ff2, final_g):
    f32 = jnp.float32
    y_p = x_prompt
    y_s = x_sample
    s_zero = jnp.zeros((x_prompt.shape[0], A_HEADS, A_DK, A_DV), f32)
    ctx_states = []
    for l in range(DEPTH):
        lp = (norm1_g[l], w_in[l], w_a_gate[l], b_a_gate[l], a_norm_g[l], w_dw[l],
              b_ln_g[l], b_ln_b[l], c_ln_g[l], c_ln_b[l], w_s[l], b_s[l], w_out[l],
              norm2_g[l], w_ff1[l], w_ff2[l])
        mod_ctx = adaln(c_ctx, w_mod[l], b_mod[l])
        y_p, s_f, s_b = layer(y_p, mod_ctx, lp, s_zero, s_zero, False)
        ctx_states.append(jnp.stack([s_f, s_b], axis=1))
        mod_lat = adaln(c, w_mod[l], b_mod[l])
        y_s, _, _ = layer(y_s, mod_lat, lp,
                          state_gla[:, l, 0].astype(f32), state_gla[:, l, 1].astype(f32), True)
    new_state_gla = jnp.stack(ctx_states, axis=1).astype(state_gla.dtype)
    y_prompt = rmsnorm(y_p, final_g)
    y_sample = rmsnorm(y_s, final_g)
    return (y_prompt, y_sample, new_state_gla)
```

```python
import functools

import jax
import jax.numpy as jnp
from jax import lax
from jax.experimental import pallas as pl
from jax.experimental.pallas import tpu as pltpu

F32 = jnp.float32
BF16 = jnp.bfloat16

D_MODEL = 1024
EPS = 1e-6
N_MOD = 6
A_HEADS = 4
A_DK = 64
A_DV = 128
A_KW = A_HEADS * A_DK
A_W = A_HEADS * A_DV
A_RANK = 16
A_TAU = 16.0
A_CHUNK = 64
B_W = 256
B_KSIZE = 31
B_PAD = 16
GRID_W = 64
C_W = 256
C_HEADS = 4
C_HD = 64
C_CHUNK = 128
D_FF = 4096
LR_PAD = 128

KV_W = A_KW + A_W
REST_W = A_KW + A_W + 2 * B_W + 2 * C_W
IN_W = KV_W + LR_PAD + REST_W

TOK_TILE = 256
FF_CHUNK = 1024
VMEM_LIMIT = 56 * 1024 * 1024


def _dot(a, b):
    return jnp.dot(a, b, preferred_element_type=F32)


def _const_spec(shape):
    zeros = (0,) * len(shape)
    return pl.BlockSpec(shape, lambda *_: zeros, pipeline_mode=pl.Buffered(1))


def _mod_kernel(cond_ref, w_ref, b_ref, out_ref):
    c = cond_ref[...]
    s = c * jax.nn.sigmoid(c)
    out_ref[0] = _dot(s.astype(BF16), w_ref[0].astype(BF16)) + b_ref[0]


def _modulation(cond, w_mod, b_mod):
    n_layers = w_mod.shape[0]
    rows = cond.shape[0]
    return pl.pallas_call(
        _mod_kernel,
        grid=(n_layers, N_MOD),
        in_specs=[
            pl.BlockSpec((rows, D_MODEL), lambda l, j: (0, 0)),
            pl.BlockSpec((1, D_MODEL, D_MODEL), lambda l, j: (l, 0, j)),
            pl.BlockSpec((1, 1, D_MODEL), lambda l, j: (l, 0, j)),
        ],
        out_specs=pl.BlockSpec((1, rows, D_MODEL), lambda l, j: (l, 0, j)),
        out_shape=jax.ShapeDtypeStruct((n_layers, rows, N_MOD * D_MODEL), F32),
        name="adaln_modulation",
    )(cond, w_mod, b_mod.reshape(n_layers, 1, N_MOD * D_MODEL))


def _rms(x, g):
    ms = jnp.mean(x * x, axis=-1, keepdims=True)
    return x * lax.rsqrt(ms + EPS) * g


def _front_kernel(x_ref, mod_ref, g_ref, w_ref, kv_ref, lr_ref, rest_ref):
    m = mod_ref[0]
    h = _rms(x_ref[...], g_ref[...]) * (1.0 + m[1:2]) + m[0:1]
    hb = h.astype(BF16)
    kv_ref[...] = _dot(hb, w_ref[:, 0:KV_W])
    lr_ref[...] = _dot(hb, w_ref[:, KV_W:KV_W + LR_PAD])
    rest_ref[...] = _dot(hb, w_ref[:, KV_W + LR_PAD:IN_W])


def _front(x, mod, mod_row, norm_g, w_in):
    n_tok = x.shape[0]
    tile = TOK_TILE
    return pl.pallas_call(
        _front_kernel,
        grid=(n_tok // tile,),
        in_specs=[
            pl.BlockSpec((tile, D_MODEL), lambda i: (i, 0)),
            pl.BlockSpec((1, N_MOD, D_MODEL), lambda i: (mod_row(i), 0, 0)),
            _const_spec((1, D_MODEL)),
            _const_spec((D_MODEL, IN_W)),
        ],
        out_specs=[
            pl.BlockSpec((tile, KV_W), lambda i: (i, 0)),
            pl.BlockSpec((tile, LR_PAD), lambda i: (i, 0)),
            pl.BlockSpec((tile, REST_W), lambda i: (i, 0)),
        ],
        out_shape=[
            jax.ShapeDtypeStruct((n_tok, KV_W), F32),
            jax.ShapeDtypeStruct((n_tok, LR_PAD), F32),
            jax.ShapeDtypeStruct((n_tok, REST_W), F32),
        ],
        compiler_params=pltpu.CompilerParams(
            dimension_semantics=("arbitrary",), vmem_limit_bytes=VMEM_LIMIT),
        name="front_norm_inproj",
    )(x, mod, norm_g, w_in)


def _log_sigmoid(x):
    return jnp.minimum(x, 0.0) - jnp.log1p(jnp.exp(-jnp.abs(x)))


def _tri_cumsum(tri, x):
    hi = x.astype(BF16)
    r1 = x - hi.astype(F32)
    mid = r1.astype(BF16)
    lo = (r1 - mid.astype(F32)).astype(BF16)
    return _dot(tri, hi) + _dot(tri, mid) + _dot(tri, lo)


def _head_stack(x, head_of_lane):
    zero = jnp.zeros_like(x)
    return jnp.concatenate(
        [jnp.where(head_of_lane == h, x, zero) for h in range(A_HEADS)], axis=0)


def _v_stack(v):
    return jnp.concatenate(
        [v[:, h * A_DV:(h + 1) * A_DV] for h in range(A_HEADS)], axis=0)


def _col_bcast(row):
    return jnp.broadcast_to(row, (A_DV, A_KW)).T


def _state_update(state, k_end, v, decay_row, head_of_lane):
    ks = _head_stack(k_end.astype(BF16), head_of_lane)
    vs = _v_stack(v.astype(BF16))
    ds = lax.dot_general(ks, vs, (((0,), (0,)), ((), ())), preferred_element_type=F32)
    return _col_bcast(decay_row) * state + ds


def _layernorm(x, g, b):
    mu = jnp.mean(x, axis=-1, keepdims=True)
    xc = x - mu
    var = jnp.mean(xc * xc, axis=-1, keepdims=True)
    return xc * lax.rsqrt(var + EPS) * g + b


def _mixer_kernel(kv_ref, lr_ref, rest_ref, s0_ref, wg_ref, bg_ref, ang_ref, wdw_ref,
                  blg_ref, blb_ref, clg_ref, clb_ref, ws_ref, bs_ref,
                  mix_ref, sfin_ref, sbin_ref, srun_ref, cbuf_ref, *, n_tiles, conv_w):
    phase = pl.program_id(1)
    t = pl.program_id(2)
    n_chunks = TOK_TILE // A_CHUNK
    L = A_CHUNK

    row = lax.broadcasted_iota(jnp.int32, (L, L), 0)
    col = lax.broadcasted_iota(jnp.int32, (L, L), 1)
    ltri = (row >= col).astype(BF16)
    utri = (col >= row).astype(BF16)
    head_of_lane = lax.broadcasted_iota(jnp.int32, (1, A_KW), 1) // A_DK

    def gate_logs(lr, lo, hi):
        logit = _dot(lr.astype(BF16), wg_ref[:, lo:hi]) + bg_ref[:, lo:hi]
        return _log_sigmoid(logit) * (1.0 / A_TAU)

    @pl.when(phase == 0)
    def _():
        @pl.when(t == 0)
        def _():
            srun_ref[1] = s0_ref[0, 1]

        tile = n_tiles - 1 - t

        def body(i, carry):
            ci = n_chunks - 1 - i
            rows = pl.ds(pl.multiple_of(ci * L, L), L)
            k = kv_ref[rows, 0:A_KW]
            v = kv_ref[rows, A_KW:KV_W]
            la = gate_logs(lr_ref[rows, :], A_KW, 2 * A_KW)
            b_suf = _tri_cumsum(utri, la)
            tot = b_suf[0:1, :]
            state = srun_ref[1]
            sbin_ref[tile * n_chunks + ci] = state
            k_end = k * jnp.exp(tot - b_suf)
            srun_ref[1] = _state_update(state, k_end, v, jnp.exp(tot), head_of_lane)
            return carry

        lax.fori_loop(0, n_chunks, body, 0)

        @pl.when(t == n_tiles - 1)
        def _():
            sfin_ref[0, 1] = srun_ref[1]

    @pl.when(phase == 1)
    def _():
        @pl.when(t == 0)
        def _():
            srun_ref[0] = s0_ref[0, 0]

        lower = row >= col
        upper = col >= row
        lower4 = jnp.concatenate([lower] * A_HEADS, axis=0)
        upper4 = jnp.concatenate([upper] * A_HEADS, axis=0)

        def body(ci, carry):
            rows = pl.ds(pl.multiple_of(ci * L, L), L)
            k = kv_ref[rows, 0:A_KW]
            v = kv_ref[rows, A_KW:KV_W]
            q = rest_ref[rows, 0:A_KW] * (A_DK ** -0.5)
            g = rest_ref[rows, A_KW:A_KW + A_W]
            la = gate_logs(lr_ref[rows, :], 0, 2 * A_KW)
            b_pre = _tri_cumsum(ltri, la[:, 0:A_KW])
            b_suf = _tri_cumsum(utri, la[:, A_KW:2 * A_KW])
            tot_f = b_pre[L - 1:L, :]

            qf = (q * jnp.exp(b_pre)).astype(BF16)
            kf = (k * jnp.exp(-b_pre)).astype(BF16)
            qb = (q * jnp.exp(b_suf)).astype(BF16)
            kb = (k * jnp.exp(-b_suf)).astype(BF16)
            qf4 = _head_stack(qf, head_of_lane)
            qb4 = _head_stack(qb, head_of_lane)
            nt = (((1,), (1,)), ((), ()))
            att_f = lax.dot_general(qf4, kf, nt, preferred_element_type=F32)
            att_b = lax.dot_general(qb4, kb, nt, preferred_element_type=F32)
            att = (jnp.where(lower4, att_f, 0.0) + jnp.where(upper4, att_b, 0.0)).astype(BF16)

            s_f = srun_ref[0]
            s_b = sbin_ref[t * n_chunks + ci]
            s_cat = jnp.concatenate([s_f, s_b], axis=0).astype(BF16)
            q_cat = jnp.concatenate([qf4, qb4], axis=1)
            vb = v.astype(BF16)
            ang = ang_ref[...]
            for h in range(A_HEADS):
                hs = slice(h * L, (h + 1) * L)
                vs = slice(h * A_DV, (h + 1) * A_DV)
                o = _dot(att[hs], vb[:, vs]) + _dot(q_cat[hs], s_cat)
                o = _rms(o, ang)
                gh = g[:, vs]
                mix_ref[rows, vs] = (o * (gh * jax.nn.sigmoid(gh))).astype(BF16)

            k_end = k * jnp.exp(tot_f - b_pre)
            srun_ref[0] = _state_update(s_f, k_end, v, jnp.exp(tot_f), head_of_lane)
            return carry

        lax.fori_loop(0, n_chunks, body, 0)

        @pl.when(t == n_tiles - 1)
        def _():
            sfin_ref[0, 0] = srun_ref[0]

        glu_lo = A_KW + A_W
        n_rows = TOK_TILE // conv_w
        for r in range(n_rows):
            rs = slice(r * conv_w, (r + 1) * conv_w)
            a = rest_ref[rs, glu_lo:glu_lo + B_W]
            b = rest_ref[rs, glu_lo + B_W:glu_lo + 2 * B_W]
            cbuf_ref[r, 0:B_PAD, :] = jnp.zeros((B_PAD, B_W), F32)
            cbuf_ref[r, B_PAD + conv_w:2 * B_PAD + conv_w, :] = jnp.zeros((B_PAD, B_W), F32)
            cbuf_ref[r, B_PAD:B_PAD + conv_w, :] = a * jax.nn.sigmoid(b)
        blk = GRID_W
        base = B_PAD - B_KSIZE // 2
        for i in range(TOK_TILE // blk):
            r, off = divmod(i * blk, conv_w)
            acc = jnp.zeros((blk, B_W), F32)
            for j in range(B_KSIZE):
                lo = base + off + j
                acc = acc + wdw_ref[j:j + 1, :] * cbuf_ref[r, lo:lo + blk, :]
            y = _layernorm(acc, blg_ref[...], blb_ref[...])
            mix_ref[i * blk:(i + 1) * blk, A_W:A_W + B_W] = (y * jax.nn.sigmoid(y)).astype(BF16)

        uv_lo = glu_lo + 2 * B_W
        head_of_lane_c = lax.broadcasted_iota(jnp.int32, (1, C_W), 1) // C_HD
        for n in range(TOK_TILE // C_CHUNK):
            rs = slice(n * C_CHUNK, (n + 1) * C_CHUNK)
            ge = jax.nn.gelu(rest_ref[rs, uv_lo:uv_lo + 2 * C_W])
            u = ge[:, 0:C_W]
            vn = _layernorm(ge[:, C_W:2 * C_W], clg_ref[...], clb_ref[...]).astype(BF16)
            sv = bs_ref[...]
            for h in range(C_HEADS):
                sv = sv + jnp.where(head_of_lane_c == h, _dot(ws_ref[h], vn), 0.0)
            mix_ref[rs, A_W + B_W:A_W + B_W + C_W] = (u * sv).astype(BF16)


def _mixer(kv, lr, rest, s0, wts, *, batch, seq, conv_w):
    n_tok = kv.shape[0]
    tile = TOK_TILE
    n_tiles = seq // tile
    n_rows = tile // conv_w
    (w_gate, b_gate, a_norm_g, w_dw, b_ln_g, b_ln_b, c_ln_g, c_ln_b, w_s, b_s) = wts

    def sweep_idx(b, p, t):
        return (b * n_tiles + t + (1 - p) * (n_tiles - 1 - 2 * t), 0)

    def fwd_idx(b, p, t):
        return (b * n_tiles + p * t, 0)

    state_spec = pl.BlockSpec((1, 2, A_KW, A_DV), lambda b, p, t: (b, 0, 0, 0))
    return pl.pallas_call(
        functools.partial(_mixer_kernel, n_tiles=n_tiles, conv_w=conv_w),
        grid=(batch, 2, n_tiles),
        in_specs=[
            pl.BlockSpec((tile, KV_W), sweep_idx),
            pl.BlockSpec((tile, LR_PAD), sweep_idx),
            pl.BlockSpec((tile, REST_W), fwd_idx),
            state_spec,
            _const_spec(w_gate.shape), _const_spec(b_gate.shape), _const_spec(a_norm_g.shape),
            _const_spec(w_dw.shape), _const_spec(b_ln_g.shape), _const_spec(b_ln_b.shape),
            _const_spec(c_ln_g.shape), _const_spec(c_ln_b.shape), _const_spec(w_s.shape),
            _const_spec(b_s.shape),
        ],
        out_specs=[pl.BlockSpec((tile, D_MODEL), fwd_idx), state_spec],
        out_shape=[
            jax.ShapeDtypeStruct((n_tok, D_MODEL), BF16),
            jax.ShapeDtypeStruct((batch, 2, A_KW, A_DV), F32),
        ],
        scratch_shapes=[
            pltpu.VMEM((seq // A_CHUNK, A_KW, A_DV), F32),
            pltpu.VMEM((2, A_KW, A_DV), F32),
            pltpu.VMEM((n_rows, conv_w + 2 * B_PAD, B_W), F32),
        ],
        compiler_params=pltpu.CompilerParams(
            dimension_semantics=("arbitrary", "arbitrary", "arbitrary"),
            vmem_limit_bytes=VMEM_LIMIT),
        name="mixer_gla_conv_gmlp",
    )(kv, lr, rest, s0, w_gate, b_gate, a_norm_g, w_dw, b_ln_g, b_ln_b, c_ln_g, c_ln_b, w_s, b_s)


def _back_kernel(x_ref, mix_ref, mod_ref, wout_ref, g2_ref, w1_ref, w2_ref, fg_ref, out_ref,
                 *, final):
    m = mod_ref[0]
    x1 = x_ref[...] + m[2:3] * _dot(mix_ref[...], wout_ref[...])
    hb = (_rms(x1, g2_ref[...]) * (1.0 + m[4:5]) + m[3:4]).astype(BF16)
    acc = jnp.zeros(x1.shape, F32)
    for c in range(D_FF // FF_CHUNK):
        cs = slice(c * FF_CHUNK, (c + 1) * FF_CHUNK)
        a = jnp.maximum(_dot(hb, w1_ref[:, cs]), 0.0)
        acc = acc + _dot((a * a).astype(BF16), w2_ref[cs, :])
    x2 = x1 + m[5:6] * acc
    if final:
        x2 = _rms(x2, fg_ref[...])
    out_ref[...] = x2


def _back(x, mix, mod, mod_row, w_out, norm2_g, w_ff1, w_ff2, final_g, *, final):
    n_tok = x.shape[0]
    tile = TOK_TILE
    return pl.pallas_call(
        functools.partial(_back_kernel, final=final),
        grid=(n_tok // tile,),
        in_specs=[
            pl.BlockSpec((tile, D_MODEL), lambda i: (i, 0)),
            pl.BlockSpec((tile, D_MODEL), lambda i: (i, 0)),
            pl.BlockSpec((1, N_MOD, D_MODEL), lambda i: (mod_row(i), 0, 0)),
            _const_spec(w_out.shape), _const_spec(norm2_g.shape),
            _const_spec(w_ff1.shape), _const_spec(w_ff2.shape), _const_spec(final_g.shape),
        ],
        out_specs=pl.BlockSpec((tile, D_MODEL), lambda i: (i, 0)),
        out_shape=jax.ShapeDtypeStruct((n_tok, D_MODEL), F32),
        compiler_params=pltpu.CompilerParams(
            dimension_semantics=("arbitrary",), vmem_limit_bytes=VMEM_LIMIT),
        name="back_outproj_ffn",
    )(x, mix, mod, w_out, norm2_g, w_ff1, w_ff2, final_g)


def _prep_layer(l, norm1_g, w_in, w_a_gate, b_a_gate, a_norm_g, w_dw, b_ln_g, b_ln_b,
                c_ln_g, c_ln_b, w_s, b_s, w_out, norm2_g, w_ff1, w_ff2):
    w = w_in[l]
    q0, k0, v0, g0 = 0, A_KW, 2 * A_KW, 2 * A_KW + A_W
    lr0 = g0 + A_W
    glu0 = lr0 + 2 * A_RANK
    uv0 = glu0 + 2 * B_W
    lr_cols = jnp.pad(w[:, lr0:glu0], ((0, 0), (0, LR_PAD - 2 * A_RANK)))
    w_in_p = jnp.concatenate(
        [w[:, k0:v0], w[:, v0:g0], lr_cols, w[:, q0:k0], w[:, g0:lr0], w[:, glu0:uv0],
         w[:, uv0:uv0 + 2 * C_W]], axis=1).astype(BF16)
    w_gate = jnp.zeros((LR_PAD, 2 * A_KW), F32)
    w_gate = w_gate.at[0:A_RANK, 0:A_KW].set(w_a_gate[l, 0])
    w_gate = w_gate.at[A_RANK:2 * A_RANK, A_KW:2 * A_KW].set(w_a_gate[l, 1]).astype(BF16)
    mixer_w = (
        w_gate,
        b_a_gate[l].reshape(1, 2 * A_KW),
        a_norm_g[l].reshape(1, A_DV),
        jnp.pad(w_dw[l].reshape(B_KSIZE, B_W), ((0, 1), (0, 0))),
        b_ln_g[l].reshape(1, B_W), b_ln_b[l].reshape(1, B_W),
        c_ln_g[l].reshape(1, C_W), c_ln_b[l].reshape(1, C_W),
        w_s[l].astype(BF16),
        jnp.repeat(b_s[l].T, C_HD, axis=1),
    )
    back_w = (w_out[l].astype(BF16), norm2_g[l].reshape(1, D_MODEL),
              w_ff1[l].astype(BF16), w_ff2[l].astype(BF16))
    return norm1_g[l].reshape(1, D_MODEL), w_in_p, mixer_w, back_w


def _layer(x, mod, mod_row, s0, front_w, mixer_w, back_w, final_g, *, batch, seq, conv_w, final):
    norm1_g, w_in_p = front_w
    kv, lr, rest = _front(x, mod, mod_row, norm1_g, w_in_p)
    mix, s_fin = _mixer(kv, lr, rest, s0, mixer_w, batch=batch, seq=seq, conv_w=conv_w)
    x = _back(x, mix, mod, mod_row, *back_w, final_g, final=final)
    return x, s_fin


def kernel(x_prompt, x_sample, c, state_gla, c_ctx, w_mod, b_mod, norm1_g, w_in, w_a_gate,
           b_a_gate, a_norm_g, w_dw, b_ln_g, b_ln_b, c_ln_g, c_ln_b, w_s, b_s, w_out, norm2_g,
           w_ff1, w_ff2, final_g):
    n_layers = w_in.shape[0]
    n_ctx, ctx_len, _ = x_prompt.shape
    n_lat, lat_len, _ = x_sample.shape
    assert ctx_len == TOK_TILE and lat_len % TOK_TILE == 0

    cond_rows = 16
    ctx_row = n_lat
    cond = jnp.zeros((cond_rows, D_MODEL), F32).at[0:n_lat].set(c).at[ctx_row].set(c_ctx)
    mod = _modulation(cond, w_mod, b_mod).reshape(n_layers, cond_rows, N_MOD, D_MODEL)

    lat_tiles = lat_len // TOK_TILE
    y_p = x_prompt.reshape(n_ctx * ctx_len, D_MODEL)
    y_s = x_sample.reshape(n_lat * lat_len, D_MODEL)
    s_zero = jnp.zeros((n_ctx, 2, A_KW, A_DV), F32)
    fg = final_g.reshape(1, D_MODEL)
    ctx_states = []
    for l in range(n_layers):
        norm1, w_in_p, mixer_w, back_w = _prep_layer(
            l, norm1_g, w_in, w_a_gate, b_a_gate, a_norm_g, w_dw, b_ln_g, b_ln_b,
            c_ln_g, c_ln_b, w_s, b_s, w_out, norm2_g, w_ff1, w_ff2)
        final = l == n_layers - 1
        y_p, s_ctx = _layer(y_p, mod[l], lambda i: ctx_row, s_zero, (norm1, w_in_p), mixer_w,
                            back_w, fg, batch=n_ctx, seq=ctx_len, conv_w=ctx_len, final=final)
        ctx_states.append(s_ctx.reshape(n_ctx, 2, A_HEADS, A_DK, A_DV))
        s_lat = state_gla[:, l].astype(F32).reshape(n_lat, 2, A_KW, A_DV)
        y_s, _ = _layer(y_s, mod[l], lambda i: i // lat_tiles, s_lat, (norm1, w_in_p), mixer_w,
                        back_w, fg, batch=n_lat, seq=lat_len, conv_w=GRID_W, final=final)
    new_state = jnp.stack(ctx_states, axis=1).astype(state_gla.dtype)
    return (y_p.reshape(x_prompt.shape), y_s.reshape(x_sample.shape), new_state)
```

```python
import functools

import jax
import jax.numpy as jnp
from jax import lax
from jax.experimental import pallas as pl
from jax.experimental.pallas import tpu as pltpu

F32 = jnp.float32
BF16 = jnp.bfloat16

SUBLANES = 8
D_MODEL = 1024
EPS = 1e-6
N_MOD = 6
A_HEADS = 4
A_DK = 64
A_DV = 128
A_KW = A_HEADS * A_DK
A_W = A_HEADS * A_DV
A_RANK = 16
A_TAU = 16.0
A_CHUNK = 64
B_W = 256
B_KSIZE = 31
B_PAD = 16
GRID_W = 64
C_W = 256
C_HEADS = 4
C_HD = 64
C_CHUNK = 128
D_FF = 4096
LR_PAD = 128

KV_W = A_KW + A_W
REST_W = A_KW + A_W + 2 * B_W + 2 * C_W
IN_W = KV_W + LR_PAD + REST_W

TOK_TILE = 256
FF_CHUNK = 1024
VMEM_LIMIT = 56 * 1024 * 1024


def _dot(a, b):
    return jnp.dot(a, b, preferred_element_type=F32)


def _const_spec(shape):
    zeros = (0,) * len(shape)
    return pl.BlockSpec(shape, lambda *_: zeros, pipeline_mode=pl.Buffered(1))


def _mod_kernel(cond_ref, w_ref, b_ref, out_ref):
    c = cond_ref[...]
    s = c * jax.nn.sigmoid(c)
    out_ref[0] = _dot(s.astype(BF16), w_ref[0].astype(BF16)) + b_ref[0]


def _modulation(cond, w_mod, b_mod):
    n_layers = w_mod.shape[0]
    rows = cond.shape[0]
    return pl.pallas_call(
        _mod_kernel,
        grid=(n_layers, N_MOD),
        in_specs=[
            pl.BlockSpec((rows, D_MODEL), lambda l, j: (0, 0)),
            pl.BlockSpec((1, D_MODEL, D_MODEL), lambda l, j: (l, 0, j)),
            pl.BlockSpec((1, 1, D_MODEL), lambda l, j: (l, 0, j)),
        ],
        out_specs=pl.BlockSpec((1, rows, D_MODEL), lambda l, j: (l, 0, j)),
        out_shape=jax.ShapeDtypeStruct((n_layers, rows, N_MOD * D_MODEL), F32),
        name="adaln_modulation",
    )(cond, w_mod, b_mod.reshape(n_layers, 1, N_MOD * D_MODEL))


def _rms(x, g):
    ms = jnp.mean(x * x, axis=-1, keepdims=True)
    return x * lax.rsqrt(ms + EPS) * g


def _front_kernel(x_ref, mod_ref, g_ref, w_ref, kv_ref, lr_ref, rest_ref):
    m = mod_ref[0]
    h = _rms(x_ref[...], g_ref[...]) * (1.0 + m[1:2]) + m[0:1]
    hb = h.astype(BF16)
    kv_ref[...] = _dot(hb, w_ref[:, 0:KV_W])
    lr_ref[...] = _dot(hb, w_ref[:, KV_W:KV_W + LR_PAD])
    rest_ref[...] = _dot(hb, w_ref[:, KV_W + LR_PAD:IN_W])


def _front(x, mod, mod_row, norm_g, w_in):
    n_tok = x.shape[0]
    tile = TOK_TILE
    return pl.pallas_call(
        _front_kernel,
        grid=(n_tok // tile,),
        in_specs=[
            pl.BlockSpec((tile, D_MODEL), lambda i: (i, 0)),
            pl.BlockSpec((1, N_MOD, D_MODEL), lambda i: (mod_row(i), 0, 0)),
            _const_spec((1, D_MODEL)),
            _const_spec((D_MODEL, IN_W)),
        ],
        out_specs=[
            pl.BlockSpec((tile, KV_W), lambda i: (i, 0)),
            pl.BlockSpec((tile, LR_PAD), lambda i: (i, 0)),
            pl.BlockSpec((tile, REST_W), lambda i: (i, 0)),
        ],
        out_shape=[
            jax.ShapeDtypeStruct((n_tok, KV_W), F32),
            jax.ShapeDtypeStruct((n_tok, LR_PAD), F32),
            jax.ShapeDtypeStruct((n_tok, REST_W), F32),
        ],
        compiler_params=pltpu.CompilerParams(
            dimension_semantics=("arbitrary",), vmem_limit_bytes=VMEM_LIMIT),
        name="front_norm_inproj",
    )(x, mod, norm_g, w_in)


def _log_sigmoid(x):
    return jnp.minimum(x, 0.0) - jnp.log(1.0 + jnp.exp(-jnp.abs(x)))


def _tri_cumsum(tri, x):
    hi = x.astype(BF16)
    r1 = x - hi.astype(F32)
    mid = r1.astype(BF16)
    lo = (r1 - mid.astype(F32)).astype(BF16)
    return _dot(tri, hi) + _dot(tri, mid) + _dot(tri, lo)


def _head_stack(x, head_of_lane):
    zero = jnp.zeros_like(x)
    return jnp.concatenate(
        [jnp.where(head_of_lane == h, x, zero) for h in range(A_HEADS)], axis=0)


def _v_stack(v):
    return jnp.concatenate(
        [v[:, h * A_DV:(h + 1) * A_DV] for h in range(A_HEADS)], axis=0)


def _col_bcast(row):
    return jnp.broadcast_to(row, (A_DV, A_KW)).T


def _chunk_increments(k_end, v, head_of_lane):
    ks = k_end.astype(BF16)
    vs = v.astype(BF16)
    tn = (((0,), (0,)), ((), ()))
    out = []
    for c in range(TOK_TILE // A_CHUNK):
        rs = slice(c * A_CHUNK, (c + 1) * A_CHUNK)
        out.append(lax.dot_general(_head_stack(ks[rs], head_of_lane), _v_stack(vs[rs]), tn,
                                   preferred_element_type=F32))
    return out


def _per_chunk_rows(x, offset):
    n = TOK_TILE // A_CHUNK
    return jnp.concatenate(
        [jnp.broadcast_to(x[c * A_CHUNK + offset:c * A_CHUNK + offset + 1], (A_CHUNK, x.shape[1]))
         for c in range(n)], axis=0)


def _layernorm(x, g, b):
    mu = jnp.mean(x, axis=-1, keepdims=True)
    xc = x - mu
    var = jnp.mean(xc * xc, axis=-1, keepdims=True)
    return xc * lax.rsqrt(var + EPS) * g + b


def _mixer_kernel(kv_ref, lr_ref, rest_ref, s0_ref, wg_ref, bg_ref, ang_ref, wdw_ref,
                  blg_ref, blb_ref, clg_ref, clb_ref, ws_ref, bs_ref,
                  mix_ref, sfin_ref, sbin_ref, srun_ref, cbuf_ref, *, n_tiles, conv_w):
    phase = pl.program_id(1)
    t = pl.program_id(2)
    n_chunks = TOK_TILE // A_CHUNK
    L = A_CHUNK

    row = lax.broadcasted_iota(jnp.int32, (TOK_TILE, TOK_TILE), 0)
    col = lax.broadcasted_iota(jnp.int32, (TOK_TILE, TOK_TILE), 1)
    same_chunk = (row // L) == (col // L)
    head_of_lane = lax.broadcasted_iota(jnp.int32, (1, A_KW), 1) // A_DK

    def gate_logs(lo, hi):
        logit = _dot(lr_ref[...].astype(BF16), wg_ref[:, lo:hi]) + bg_ref[:, lo:hi]
        return _log_sigmoid(logit) * (1.0 / A_TAU)

    def suffix_sums(la_b):
        utri = jnp.where(same_chunk & (col >= row), 1.0, 0.0).astype(BF16)
        return _tri_cumsum(utri, la_b)

    @pl.when(phase == 0)
    def _():
        @pl.when(t == 0)
        def _():
            srun_ref[1] = s0_ref[0, 1]

        tile = n_tiles - 1 - t
        k = kv_ref[:, 0:A_KW]
        v = kv_ref[:, A_KW:KV_W]
        b_suf = suffix_sums(gate_logs(A_KW, 2 * A_KW))
        tot = _per_chunk_rows(b_suf, 0)
        ds = _chunk_increments(k * jnp.exp(tot - b_suf), v, head_of_lane)
        state = srun_ref[1]
        for c in reversed(range(n_chunks)):
            sbin_ref[tile * n_chunks + c] = state
            state = _col_bcast(jnp.exp(b_suf[c * L:c * L + 1])) * state + ds[c]
        srun_ref[1] = state

        @pl.when(t == n_tiles - 1)
        def _():
            sfin_ref[0, 1] = state

    @pl.when(phase == 1)
    def _():
        @pl.when(t == 0)
        def _():
            srun_ref[0] = s0_ref[0, 0]

        k = kv_ref[:, 0:A_KW]
        v = kv_ref[:, A_KW:KV_W]
        q = rest_ref[:, 0:A_KW] * (A_DK ** -0.5)
        la = gate_logs(0, 2 * A_KW)
        ltri = jnp.where(same_chunk & (row >= col), 1.0, 0.0).astype(BF16)
        b_pre = _tri_cumsum(ltri, la[:, 0:A_KW])
        b_suf = suffix_sums(la[:, A_KW:2 * A_KW])
        tot_f = _per_chunk_rows(b_pre, L - 1)

        qf = (q * jnp.exp(b_pre)).astype(BF16)
        kf = (k * jnp.exp(-b_pre)).astype(BF16)
        qb = (q * jnp.exp(b_suf)).astype(BF16)
        kb = (k * jnp.exp(-b_suf)).astype(BF16)
        ds = _chunk_increments(k * jnp.exp(tot_f - b_pre), v, head_of_lane)
        vb = v.astype(BF16)

        row_c = lax.broadcasted_iota(jnp.int32, (A_HEADS * L, L), 0) % L
        col_c = lax.broadcasted_iota(jnp.int32, (A_HEADS * L, L), 1)
        nt = (((1,), (1,)), ((), ()))
        ang = ang_ref[...]
        state = srun_ref[0]
        for c in range(n_chunks):
            rs = slice(c * L, (c + 1) * L)
            qf4 = _head_stack(qf[rs], head_of_lane)
            qb4 = _head_stack(qb[rs], head_of_lane)
            att_f = lax.dot_general(qf4, kf[rs], nt, preferred_element_type=F32)
            att_b = lax.dot_general(qb4, kb[rs], nt, preferred_element_type=F32)
            att = (jnp.where(row_c >= col_c, att_f, 0.0)
                   + jnp.where(col_c >= row_c, att_b, 0.0)).astype(BF16)
            s_cat = jnp.concatenate([state, sbin_ref[t * n_chunks + c]], axis=0).astype(BF16)
            o = _dot(jnp.concatenate([qf4, qb4], axis=1), s_cat)
            for h in range(A_HEADS):
                hs = slice(h * L, (h + 1) * L)
                vs = slice(h * A_DV, (h + 1) * A_DV)
                oh = _rms(o[hs] + _dot(att[hs], vb[rs, vs]), ang)
                gh = rest_ref[rs, A_KW + h * A_DV:A_KW + (h + 1) * A_DV]
                mix_ref[rs, vs] = (oh * (gh * jax.nn.sigmoid(gh))).astype(BF16)
            state = _col_bcast(jnp.exp(b_pre[c * L + L - 1:c * L + L])) * state + ds[c]
        srun_ref[0] = state

        @pl.when(t == n_tiles - 1)
        def _():
            sfin_ref[0, 0] = state

        glu_lo = A_KW + A_W
        n_rows = TOK_TILE // conv_w
        for r in range(n_rows):
            rs = slice(r * conv_w, (r + 1) * conv_w)
            a = rest_ref[rs, glu_lo:glu_lo + B_W]
            b = rest_ref[rs, glu_lo + B_W:glu_lo + 2 * B_W]
            cbuf_ref[r, 0:B_PAD, :] = jnp.zeros((B_PAD, B_W), F32)
            cbuf_ref[r, B_PAD + conv_w:2 * B_PAD + conv_w, :] = jnp.zeros((B_PAD, B_W), F32)
            cbuf_ref[r, B_PAD:B_PAD + conv_w, :] = a * jax.nn.sigmoid(b)
        blk = GRID_W
        lead = B_PAD - B_KSIZE // 2
        assert lead == 1
        ext = blk + SUBLANES
        for i in range(TOK_TILE // blk):
            r, off = divmod(i * blk, conv_w)
            acc = None
            for res in range(SUBLANES):
                part = None
                for m in range(res if res else SUBLANES, B_KSIZE + 1, SUBLANES):
                    lo = off + m - res
                    term = wdw_ref[m - 1:m, :] * cbuf_ref[r, lo:lo + ext, :]
                    part = term if part is None else part + term
                part = part[res:res + blk]
                acc = part if acc is None else acc + part
            y = _layernorm(acc, blg_ref[...], blb_ref[...])
            mix_ref[i * blk:(i + 1) * blk, A_W:A_W + B_W] = (y * jax.nn.sigmoid(y)).astype(BF16)

        uv_lo = glu_lo + 2 * B_W
        head_of_lane_c = lax.broadcasted_iota(jnp.int32, (1, C_W), 1) // C_HD
        for n in range(TOK_TILE // C_CHUNK):
            rs = slice(n * C_CHUNK, (n + 1) * C_CHUNK)
            ge = jax.nn.gelu(rest_ref[rs, uv_lo:uv_lo + 2 * C_W])
            u = ge[:, 0:C_W]
            vn = _layernorm(ge[:, C_W:2 * C_W], clg_ref[...], clb_ref[...]).astype(BF16)
            sv = bs_ref[...]
            for h in range(C_HEADS):
                sv = sv + jnp.where(head_of_lane_c == h, _dot(ws_ref[h], vn), 0.0)
            mix_ref[rs, A_W + B_W:A_W + B_W + C_W] = (u * sv).astype(BF16)


def _mixer(kv, lr, rest, s0, wts, *, batch, seq, conv_w):
    n_tok = kv.shape[0]
    tile = TOK_TILE
    n_tiles = seq // tile
    n_rows = tile // conv_w
    (w_gate, b_gate, a_norm_g, w_dw, b_ln_g, b_ln_b, c_ln_g, c_ln_b, w_s, b_s) = wts

    def sweep_idx(b, p, t):
        return (b * n_tiles + t + (1 - p) * (n_tiles - 1 - 2 * t), 0)

    def fwd_idx(b, p, t):
        return (b * n_tiles + p * t, 0)

    state_spec = pl.BlockSpec((1, 2, A_KW, A_DV), lambda b, p, t: (b, 0, 0, 0))
    return pl.pallas_call(
        functools.partial(_mixer_kernel, n_tiles=n_tiles, conv_w=conv_w),
        grid=(batch, 2, n_tiles),
        in_specs=[
            pl.BlockSpec((tile, KV_W), sweep_idx),
            pl.BlockSpec((tile, LR_PAD), sweep_idx),
            pl.BlockSpec((tile, REST_W), fwd_idx),
            state_spec,
            _const_spec(w_gate.shape), _const_spec(b_gate.shape), _const_spec(a_norm_g.shape),
            _const_spec(w_dw.shape), _const_spec(b_ln_g.shape), _const_spec(b_ln_b.shape),
            _const_spec(c_ln_g.shape), _const_spec(c_ln_b.shape), _const_spec(w_s.shape),
            _const_spec(b_s.shape),
        ],
        out_specs=[pl.BlockSpec((tile, D_MODEL), fwd_idx), state_spec],
        out_shape=[
            jax.ShapeDtypeStruct((n_tok, D_MODEL), BF16),
            jax.ShapeDtypeStruct((batch, 2, A_KW, A_DV), F32),
        ],
        scratch_shapes=[
            pltpu.VMEM((seq // A_CHUNK, A_KW, A_DV), F32),
            pltpu.VMEM((2, A_KW, A_DV), F32),
            pltpu.VMEM((n_rows, conv_w + 2 * B_PAD, B_W), F32),
        ],
        compiler_params=pltpu.CompilerParams(
            dimension_semantics=("arbitrary", "arbitrary", "arbitrary"),
            vmem_limit_bytes=VMEM_LIMIT),
        name="mixer_gla_conv_gmlp",
    )(kv, lr, rest, s0, w_gate, b_gate, a_norm_g, w_dw, b_ln_g, b_ln_b, c_ln_g, c_ln_b, w_s, b_s)


def _back_kernel(x_ref, mix_ref, mod_ref, wout_ref, g2_ref, w1_ref, w2_ref, fg_ref, out_ref,
                 *, final):
    m = mod_ref[0]
    x1 = x_ref[...] + m[2:3] * _dot(mix_ref[...], wout_ref[...])
    hb = (_rms(x1, g2_ref[...]) * (1.0 + m[4:5]) + m[3:4]).astype(BF16)
    acc = jnp.zeros(x1.shape, F32)
    for c in range(D_FF // FF_CHUNK):
        cs = slice(c * FF_CHUNK, (c + 1) * FF_CHUNK)
        a = jnp.maximum(_dot(hb, w1_ref[:, cs]), 0.0)
        acc = acc + _dot((a * a).astype(BF16), w2_ref[cs, :])
    x2 = x1 + m[5:6] * acc
    if final:
        x2 = _rms(x2, fg_ref[...])
    out_ref[...] = x2


def _back(x, mix, mod, mod_row, w_out, norm2_g, w_ff1, w_ff2, final_g, *, final):
    n_tok = x.shape[0]
    tile = TOK_TILE
    return pl.pallas_call(
        functools.partial(_back_kernel, final=final),
        grid=(n_tok // tile,),
        in_specs=[
            pl.BlockSpec((tile, D_MODEL), lambda i: (i, 0)),
            pl.BlockSpec((tile, D_MODEL), lambda i: (i, 0)),
            pl.BlockSpec((1, N_MOD, D_MODEL), lambda i: (mod_row(i), 0, 0)),
            _const_spec(w_out.shape), _const_spec(norm2_g.shape),
            _const_spec(w_ff1.shape), _const_spec(w_ff2.shape), _const_spec(final_g.shape),
        ],
        out_specs=pl.BlockSpec((tile, D_MODEL), lambda i: (i, 0)),
        out_shape=jax.ShapeDtypeStruct((n_tok, D_MODEL), F32),
        compiler_params=pltpu.CompilerParams(
            dimension_semantics=("arbitrary",), vmem_limit_bytes=VMEM_LIMIT),
        name="back_outproj_ffn",
    )(x, mix, mod, w_out, norm2_g, w_ff1, w_ff2, final_g)


def _prep_layer(l, norm1_g, w_in, w_a_gate, b_a_gate, a_norm_g, w_dw, b_ln_g, b_ln_b,
                c_ln_g, c_ln_b, w_s, b_s, w_out, norm2_g, w_ff1, w_ff2):
    w = w_in[l]
    q0, k0, v0, g0 = 0, A_KW, 2 * A_KW, 2 * A_KW + A_W
    lr0 = g0 + A_W
    glu0 = lr0 + 2 * A_RANK
    uv0 = glu0 + 2 * B_W
    lr_cols = jnp.pad(w[:, lr0:glu0], ((0, 0), (0, LR_PAD - 2 * A_RANK)))
    w_in_p = jnp.concatenate(
        [w[:, k0:v0], w[:, v0:g0], lr_cols, w[:, q0:k0], w[:, g0:lr0], w[:, glu0:uv0],
         w[:, uv0:uv0 + 2 * C_W]], axis=1).astype(BF16)
    w_gate = jnp.zeros((LR_PAD, 2 * A_KW), F32)
    w_gate = w_gate.at[0:A_RANK, 0:A_KW].set(w_a_gate[l, 0])
    w_gate = w_gate.at[A_RANK:2 * A_RANK, A_KW:2 * A_KW].set(w_a_gate[l, 1]).astype(BF16)
    mixer_w = (
        w_gate,
        b_a_gate[l].reshape(1, 2 * A_KW),
        a_norm_g[l].reshape(1, A_DV),
        jnp.pad(w_dw[l].reshape(B_KSIZE, B_W), ((0, 1), (0, 0))),
        b_ln_g[l].reshape(1, B_W), b_ln_b[l].reshape(1, B_W),
        c_ln_g[l].reshape(1, C_W), c_ln_b[l].reshape(1, C_W),
        w_s[l].astype(BF16),
        jnp.repeat(b_s[l].T, C_HD, axis=1),
    )
    back_w = (w_out[l].astype(BF16), norm2_g[l].reshape(1, D_MODEL),
              w_ff1[l].astype(BF16), w_ff2[l].astype(BF16))
    return norm1_g[l].reshape(1, D_MODEL), w_in_p, mixer_w, back_w


def _layer(x, mod, mod_row, s0, front_w, mixer_w, back_w, final_g, *, batch, seq, conv_w, final):
    norm1_g, w_in_p = front_w
    kv, lr, rest = _front(x, mod, mod_row, norm1_g, w_in_p)
    mix, s_fin = _mixer(kv, lr, rest, s0, mixer_w, batch=batch, seq=seq, conv_w=conv_w)
    x = _back(x, mix, mod, mod_row, *back_w, final_g, final=final)
    return x, s_fin


def kernel(x_prompt, x_sample, c, state_gla, c_ctx, w_mod, b_mod, norm1_g, w_in, w_a_gate,
           b_a_gate, a_norm_g, w_dw, b_ln_g, b_ln_b, c_ln_g, c_ln_b, w_s, b_s, w_out, norm2_g,
           w_ff1, w_ff2, final_g):
    n_layers = w_in.shape[0]
    n_ctx, ctx_len, _ = x_prompt.shape
    n_lat, lat_len, _ = x_sample.shape
    assert ctx_len == TOK_TILE and lat_len % TOK_TILE == 0

    cond_rows = 16
    ctx_row = n_lat
    cond = jnp.zeros((cond_rows, D_MODEL), F32).at[0:n_lat].set(c).at[ctx_row].set(c_ctx)
    mod = _modulation(cond, w_mod, b_mod).reshape(n_layers, cond_rows, N_MOD, D_MODEL)

    lat_tiles = lat_len // TOK_TILE
    y_p = x_prompt.reshape(n_ctx * ctx_len, D_MODEL)
    y_s = x_sample.reshape(n_lat * lat_len, D_MODEL)
    s_zero = jnp.zeros((n_ctx, 2, A_KW, A_DV), F32)
    fg = final_g.reshape(1, D_MODEL)
    ctx_states = []
    for l in range(n_layers):
        norm1, w_in_p, mixer_w, back_w = _prep_layer(
            l, norm1_g, w_in, w_a_gate, b_a_gate, a_norm_g, w_dw, b_ln_g, b_ln_b,
            c_ln_g, c_ln_b, w_s, b_s, w_out, norm2_g, w_ff1, w_ff2)
        final = l == n_layers - 1
        y_p, s_ctx = _layer(y_p, mod[l], lambda i: ctx_row, s_zero, (norm1, w_in_p), mixer_w,
                            back_w, fg, batch=n_ctx, seq=ctx_len, conv_w=ctx_len, final=final)
        ctx_states.append(s_ctx.reshape(n_ctx, 2, A_HEADS, A_DK, A_DV))
        s_lat = state_gla[:, l].astype(F32).reshape(n_lat, 2, A_KW, A_DV)
        y_s, _ = _layer(y_s, mod[l], lambda i: i // lat_tiles, s_lat, (norm1, w_in_p), mixer_w,
                        back_w, fg, batch=n_lat, seq=lat_len, conv_w=GRID_W, final=final)
    new_state = jnp.stack(ctx_states, axis=1).astype(state_gla.dtype)
    return (y_p.reshape(x_prompt.shape), y_s.reshape(x_sample.shape), new_state)
```

```python
import functools

import jax
import jax.numpy as jnp
from jax import lax
from jax.experimental import pallas as pl
from jax.experimental.pallas import tpu as pltpu

F32 = jnp.float32
BF16 = jnp.bfloat16

SUBLANES = 8
D_MODEL = 1024
EPS = 1e-6
N_MOD = 6
A_HEADS = 4
A_DK = 64
A_DV = 128
A_KW = A_HEADS * A_DK
A_W = A_HEADS * A_DV
A_RANK = 16
A_TAU = 16.0
A_CHUNK = 64
B_W = 256
B_KSIZE = 31
B_PAD = 16
GRID_W = 64
C_W = 256
C_HEADS = 4
C_HD = 64
C_CHUNK = 128
D_FF = 4096
LR_PAD = 128

KV_W = A_KW + A_W
REST_W = A_KW + A_W + 2 * B_W + 2 * C_W
IN_W = KV_W + LR_PAD + REST_W

TOK_TILE = 256
N_CHUNKS = TOK_TILE // A_CHUNK
FF_CHUNK = 1024
VMEM_LIMIT = 56 * 1024 * 1024


def _dot(a, b):
    return jnp.dot(a, b, preferred_element_type=F32)


def _const_spec(shape):
    zeros = (0,) * len(shape)
    return pl.BlockSpec(shape, lambda *_: zeros, pipeline_mode=pl.Buffered(1))


def _mod_kernel(cond_ref, w_ref, b_ref, out_ref):
    c = cond_ref[...]
    s = c * jax.nn.sigmoid(c)
    out_ref[0] = _dot(s.astype(BF16), w_ref[0].astype(BF16)) + b_ref[0]


def _modulation(cond, w_mod, b_mod):
    n_layers = w_mod.shape[0]
    rows = cond.shape[0]
    return pl.pallas_call(
        _mod_kernel,
        grid=(n_layers, N_MOD),
        in_specs=[
            pl.BlockSpec((rows, D_MODEL), lambda l, j: (0, 0)),
            pl.BlockSpec((1, D_MODEL, D_MODEL), lambda l, j: (l, 0, j)),
            pl.BlockSpec((1, 1, D_MODEL), lambda l, j: (l, 0, j)),
        ],
        out_specs=pl.BlockSpec((1, rows, D_MODEL), lambda l, j: (l, 0, j)),
        out_shape=jax.ShapeDtypeStruct((n_layers, rows, N_MOD * D_MODEL), F32),
        name="adaln_modulation",
    )(cond, w_mod, b_mod.reshape(n_layers, 1, N_MOD * D_MODEL))


def _rms(x, g):
    ms = jnp.mean(x * x, axis=-1, keepdims=True)
    return x * lax.rsqrt(ms + EPS) * g


def _front_kernel(x_ref, mod_ref, g_ref, w_ref, kv_ref, lr_ref, rest_ref):
    m = mod_ref[0]
    h = _rms(x_ref[...], g_ref[...]) * (1.0 + m[1:2]) + m[0:1]
    hb = h.astype(BF16)
    kv_ref[...] = _dot(hb, w_ref[:, 0:KV_W])
    lr_ref[...] = _dot(hb, w_ref[:, KV_W:KV_W + LR_PAD])
    rest_ref[...] = _dot(hb, w_ref[:, KV_W + LR_PAD:IN_W])


def _front(x, mod, mod_row, norm_g, w_in):
    n_tok = x.shape[0]
    tile = TOK_TILE
    return pl.pallas_call(
        _front_kernel,
        grid=(n_tok // tile,),
        in_specs=[
            pl.BlockSpec((tile, D_MODEL), lambda i: (i, 0)),
            pl.BlockSpec((1, N_MOD, D_MODEL), lambda i: (mod_row(i), 0, 0)),
            _const_spec((1, D_MODEL)),
            _const_spec((D_MODEL, IN_W)),
        ],
        out_specs=[
            pl.BlockSpec((tile, KV_W), lambda i: (i, 0)),
            pl.BlockSpec((tile, LR_PAD), lambda i: (i, 0)),
            pl.BlockSpec((tile, REST_W), lambda i: (i, 0)),
        ],
        out_shape=[
            jax.ShapeDtypeStruct((n_tok, KV_W), F32),
            jax.ShapeDtypeStruct((n_tok, LR_PAD), F32),
            jax.ShapeDtypeStruct((n_tok, REST_W), F32),
        ],
        compiler_params=pltpu.CompilerParams(
            dimension_semantics=("arbitrary",), vmem_limit_bytes=VMEM_LIMIT),
        name="front_norm_inproj",
    )(x, mod, norm_g, w_in)


def _log_sigmoid(x):
    return jnp.minimum(x, 0.0) - jnp.log(1.0 + jnp.exp(-jnp.abs(x)))


def _gate_logs(lr_ref, wg_ref, bg_ref, lo, hi):
    logit = _dot(lr_ref[...].astype(BF16), wg_ref[:, lo:hi]) + bg_ref[:, lo:hi]
    return _log_sigmoid(logit) * (1.0 / A_TAU)


def _chunk_tri(upper, zero=0.0):
    row = lax.broadcasted_iota(jnp.int32, (TOK_TILE, TOK_TILE), 0)
    col = lax.broadcasted_iota(jnp.int32, (TOK_TILE, TOK_TILE), 1)
    same_chunk = (row // A_CHUNK) == (col // A_CHUNK)
    tri = (col >= row) if upper else (row >= col)
    return jnp.where(same_chunk & tri, 1.0 + zero, zero).astype(BF16)


def _split3(x):
    hi = x.astype(BF16)
    r1 = x - hi.astype(F32)
    mid = r1.astype(BF16)
    lo = (r1 - mid.astype(F32)).astype(BF16)
    return hi, mid, lo


def _tri_apply(tri, parts):
    hi, mid, lo = parts
    return _dot(tri, hi) + _dot(tri, mid) + _dot(tri, lo)


def _tri_cumsum(tri, x):
    return _tri_apply(tri, _split3(x))


def _pace_token(x):
    i = x[-SUBLANES:, -128:].astype(jnp.int32)
    z = lax.shift_right_logical(lax.shift_right_logical(i, 16), 16)
    return z[0:1, :].astype(F32)


def _sum_tokens(toks):
    total = toks[0]
    for tok in toks[1:]:
        total = total + tok
    return total


def _lanes(tok, width):
    return jnp.concatenate([tok] * (width // tok.shape[1]), axis=1)


def _pace_zero(pace, width):
    if len(pace["back"]) < 2:
        return jnp.zeros((1, width), F32)
    return _lanes(pace["back"][-2], width)


def _head_of_lane():
    return lax.broadcasted_iota(jnp.int32, (1, A_KW), 1) // A_DK


def _head_stack(x, head_of_lane):
    zero = jnp.zeros_like(x)
    return jnp.concatenate(
        [jnp.where(head_of_lane == h, x, zero) for h in range(A_HEADS)], axis=0)


def _v_stack(v):
    return jnp.concatenate(
        [v[:, h * A_DV:(h + 1) * A_DV] for h in range(A_HEADS)], axis=0)


def _col_bcast(row):
    return jnp.broadcast_to(row, (A_DV, A_KW)).T


def _chunk_increments(k_end, v, head_of_lane):
    ks = k_end.astype(BF16)
    vs = v.astype(BF16)
    tn = (((0,), (0,)), ((), ()))
    out = []
    for c in range(N_CHUNKS):
        rs = slice(c * A_CHUNK, (c + 1) * A_CHUNK)
        out.append(lax.dot_general(_head_stack(ks[rs], head_of_lane), _v_stack(vs[rs]), tn,
                                   preferred_element_type=F32))
    return out


def _per_chunk_rows(x, offset):
    return jnp.concatenate(
        [jnp.broadcast_to(x[c * A_CHUNK + offset:c * A_CHUNK + offset + 1], (A_CHUNK, x.shape[1]))
         for c in range(N_CHUNKS)], axis=0)


def _layernorm(x, g, b):
    mu = jnp.mean(x, axis=-1, keepdims=True)
    xc = x - mu
    var = jnp.mean(xc * xc, axis=-1, keepdims=True)
    return xc * lax.rsqrt(var + EPS) * g + b


def _bwd_state_kernel(kv_ref, lr_ref, s0_ref, wg_ref, bg_ref, sbin_ref, sfin_ref, srun_ref,
                      *, n_tiles):
    t = pl.program_id(1)
    L = A_CHUNK

    @pl.when(t == 0)
    def _():
        srun_ref[...] = s0_ref[0, 0]

    k = kv_ref[:, 0:A_KW]
    v = kv_ref[:, A_KW:KV_W]
    b_suf = _tri_cumsum(_chunk_tri(True), _gate_logs(lr_ref, wg_ref, bg_ref, A_KW, 2 * A_KW))
    tot = _per_chunk_rows(b_suf, 0)
    ds = _chunk_increments(k * jnp.exp(tot - b_suf), v, _head_of_lane())
    state = srun_ref[...]
    for c in reversed(range(N_CHUNKS)):
        sbin_ref[c] = state
        state = _col_bcast(jnp.exp(b_suf[c * L:c * L + 1])) * state + ds[c]
    srun_ref[...] = state

    @pl.when(t == n_tiles - 1)
    def _():
        sfin_ref[0] = state


def _bwd_states(kv, lr, s0, w_gate, b_gate, *, batch, seq):
    tile = TOK_TILE
    n_tiles = seq // tile

    def sweep(b, t):
        return (b * n_tiles + n_tiles - 1 - t, 0)

    return pl.pallas_call(
        functools.partial(_bwd_state_kernel, n_tiles=n_tiles),
        grid=(batch, n_tiles),
        in_specs=[
            pl.BlockSpec((tile, KV_W), sweep),
            pl.BlockSpec((tile, LR_PAD), sweep),
            pl.BlockSpec((1, 1, A_KW, A_DV), lambda b, t: (b, 1, 0, 0)),
            _const_spec(w_gate.shape), _const_spec(b_gate.shape),
        ],
        out_specs=[
            pl.BlockSpec((N_CHUNKS, A_KW, A_DV), lambda b, t: (b * n_tiles + n_tiles - 1 - t, 0, 0)),
            pl.BlockSpec((1, A_KW, A_DV), lambda b, t: (b, 0, 0)),
        ],
        out_shape=[
            jax.ShapeDtypeStruct((batch * seq // A_CHUNK, A_KW, A_DV), F32),
            jax.ShapeDtypeStruct((batch, A_KW, A_DV), F32),
        ],
        scratch_shapes=[pltpu.VMEM((A_KW, A_DV), F32)],
        compiler_params=pltpu.CompilerParams(
            dimension_semantics=("arbitrary", "arbitrary"), vmem_limit_bytes=VMEM_LIMIT),
        name="gla_bwd_states",
    )(kv, lr, s0, w_gate, b_gate)


def _back_stages(x_ref, mixbuf_ref, mod_ref, wout_ref, g2_ref, w1_ref, w2_ref, fg_ref, out_ref,
                 final, pace):
    m = mod_ref[0]
    x1 = x_ref[...] + m[2:3] * _dot(mixbuf_ref[...], wout_ref[...])
    h2 = _rms(x1, g2_ref[...]) * (1.0 + m[4:5]) + m[3:4]
    hb = h2.astype(BF16)
    acc = jnp.zeros(x1.shape, F32)
    pace["back"].append(_pace_token(h2))
    yield
    for c in range(D_FF // FF_CHUNK):
        cs = slice(c * FF_CHUNK, (c + 1) * FF_CHUNK)
        floor = 0.0
        if pace["mix"]:
            floor = _lanes(_sum_tokens(pace["mix"]), FF_CHUNK)
            pace["mix"].clear()
        a = jnp.maximum(_dot(hb, w1_ref[:, cs]), floor)
        acc = acc + _dot((a * a).astype(BF16), w2_ref[cs, :])
        pace["back"].append(_pace_token(acc))
        yield
    gate2 = m[5:6]
    if pace["mix"]:
        gate2 = gate2 + _lanes(_sum_tokens(pace["mix"]), D_MODEL)
    x2 = x1 + gate2 * acc
    if final:
        x2 = _rms(x2, fg_ref[...])
    out_ref[...] = x2
    yield


def _gla_stages(kv_ref, lr_ref, rest_ref, sbin_ref, srun_ref, wg_ref, bg_ref, ang_ref, mix_ref,
                state_out, pace):
    L = A_CHUNK
    head_of_lane = _head_of_lane()
    la = _gate_logs(lr_ref, wg_ref, bg_ref, 0, 2 * A_KW)
    parts_f = _split3(la[:, 0:A_KW])
    parts_b = _split3(la[:, A_KW:2 * A_KW])
    yield

    zero = _pace_zero(pace, TOK_TILE)
    k = kv_ref[:, 0:A_KW]
    v = kv_ref[:, A_KW:KV_W]
    q = rest_ref[:, 0:A_KW] * (A_DK ** -0.5)
    b_pre = _tri_apply(_chunk_tri(False, zero), parts_f)
    b_suf = _tri_apply(_chunk_tri(True, zero), parts_b)
    tot_f = _per_chunk_rows(b_pre, L - 1)

    qf = (q * jnp.exp(b_pre)).astype(BF16)
    kf = (k * jnp.exp(-b_pre)).astype(BF16)
    qb = (q * jnp.exp(b_suf)).astype(BF16)
    kb = (k * jnp.exp(-b_suf)).astype(BF16)
    ds = _chunk_increments(k * jnp.exp(tot_f - b_pre), v, head_of_lane)
    vb = v.astype(BF16)
    pace["mix"].append(_pace_token(ds[-1]))
    yield

    row_c = lax.broadcasted_iota(jnp.int32, (A_HEADS * L, L), 0) % L
    col_c = lax.broadcasted_iota(jnp.int32, (A_HEADS * L, L), 1)
    nt = (((1,), (1,)), ((), ()))
    ang = ang_ref[...]
    state = srun_ref[...]
    for c in range(N_CHUNKS):
        rs = slice(c * L, (c + 1) * L)
        zero = _pace_zero(pace, A_KW).astype(BF16)
        qf4 = _head_stack(qf[rs], head_of_lane)
        qb4 = _head_stack(qb[rs], head_of_lane)
        att_f = lax.dot_general(qf4, kf[rs] + zero, nt, preferred_element_type=F32)
        att_b = lax.dot_general(qb4, kb[rs] + zero, nt, preferred_element_type=F32)
        att = (jnp.where(row_c >= col_c, att_f, 0.0)
               + jnp.where(col_c >= row_c, att_b, 0.0)).astype(BF16)
        s_cat = jnp.concatenate([state, sbin_ref[c]], axis=0).astype(BF16)
        o = _dot(jnp.concatenate([qf4, qb4], axis=1), s_cat)
        for h in range(A_HEADS):
            hs = slice(h * L, (h + 1) * L)
            vs = slice(h * A_DV, (h + 1) * A_DV)
            oh = _rms(o[hs] + _dot(att[hs], vb[rs, vs]), ang)
            gh = rest_ref[rs, A_KW + h * A_DV:A_KW + (h + 1) * A_DV]
            oh = oh * (gh * jax.nn.sigmoid(gh))
            mix_ref[rs, vs] = oh.astype(BF16)
            pace["mix"].append(_pace_token(oh))
        state = _col_bcast(jnp.exp(b_pre[c * L + L - 1:c * L + L])) * state + ds[c]
        if c == N_CHUNKS - 1:
            state_out.append(state)
        yield


def _conv_stages(rest_ref, wdw_ref, blg_ref, blb_ref, cbuf_ref, mix_ref, conv_w, pace):
    glu_lo = A_KW + A_W
    for r in range(TOK_TILE // conv_w):
        rs = slice(r * conv_w, (r + 1) * conv_w)
        a = rest_ref[rs, glu_lo:glu_lo + B_W]
        b = rest_ref[rs, glu_lo + B_W:glu_lo + 2 * B_W]
        cbuf_ref[r, 0:B_PAD, :] = jnp.zeros((B_PAD, B_W), F32)
        cbuf_ref[r, B_PAD + conv_w:2 * B_PAD + conv_w, :] = jnp.zeros((B_PAD, B_W), F32)
        cbuf_ref[r, B_PAD:B_PAD + conv_w, :] = a * jax.nn.sigmoid(b)
    blk = GRID_W
    assert B_PAD - B_KSIZE // 2 == 1
    ext = blk + SUBLANES
    for i in range(TOK_TILE // blk):
        r, off = divmod(i * blk, conv_w)
        zero = _pace_zero(pace, B_W)
        acc = None
        for res in range(SUBLANES):
            part = None
            for m in range(res if res else SUBLANES, B_KSIZE + 1, SUBLANES):
                lo = off + m - res
                term = (wdw_ref[m - 1:m, :] + zero) * cbuf_ref[r, lo:lo + ext, :]
                part = term if part is None else part + term
            part = part[res:res + blk]
            acc = part if acc is None else acc + part
        y = _layernorm(acc, blg_ref[...], blb_ref[...])
        y = y * jax.nn.sigmoid(y)
        mix_ref[i * blk:(i + 1) * blk, A_W:A_W + B_W] = y.astype(BF16)
        pace["mix"].append(_pace_token(y))
        yield


def _gmlp_stages(rest_ref, clg_ref, clb_ref, ws_ref, bs_ref, mix_ref, pace):
    uv_lo = A_KW + A_W + 2 * B_W
    head_of_lane_c = lax.broadcasted_iota(jnp.int32, (1, C_W), 1) // C_HD
    for n in range(TOK_TILE // C_CHUNK):
        rs = slice(n * C_CHUNK, (n + 1) * C_CHUNK)
        ge = jax.nn.gelu(rest_ref[rs, uv_lo:uv_lo + 2 * C_W] + _pace_zero(pace, 2 * C_W))
        u = ge[:, 0:C_W]
        vn = _layernorm(ge[:, C_W:2 * C_W], clg_ref[...], clb_ref[...]).astype(BF16)
        sv = bs_ref[...]
        for h in range(C_HEADS):
            sv = sv + jnp.where(head_of_lane_c == h, _dot(ws_ref[h], vn), 0.0)
        gated = u * sv
        mix_ref[rs, A_W + B_W:A_W + B_W + C_W] = gated.astype(BF16)
        pace["mix"].append(_pace_token(gated))
        yield


_STEP_ORDER = ("gla", "back",
               "back", "gla", "conv",
               "back", "gla", "gla", "gmlp", "conv",
               "back", "gla", "gla", "gmlp", "conv",
               "back", "conv",
               "back")


def _mixback_kernel(kv_ref, lr_ref, rest_ref, sbin_ref, s0_ref, x_ref, mod_ref,
                    wg_ref, bg_ref, ang_ref, wdw_ref, blg_ref, blb_ref, clg_ref, clb_ref,
                    ws_ref, bs_ref, wout_ref, g2_ref, w1_ref, w2_ref, fg_ref,
                    out_ref, sfin_ref, mixbuf_ref, srun_ref, cbuf_ref,
                    *, n_total, n_tiles, conv_w, final):
    g = pl.program_id(0)

    @pl.when(g == 0)
    def _():
        mixbuf_ref[...] = jnp.zeros(mixbuf_ref.shape, BF16)

    @pl.when(jnp.minimum(g, n_total - 1) % n_tiles == 0)
    def _():
        srun_ref[...] = s0_ref[0, 0]

    new_state = []
    pace = {"back": [], "mix": []}
    stages = {
        "back": _back_stages(x_ref, mixbuf_ref, mod_ref, wout_ref, g2_ref, w1_ref, w2_ref, fg_ref,
                             out_ref, final, pace),
        "gla": _gla_stages(kv_ref, lr_ref, rest_ref, sbin_ref, srun_ref, wg_ref, bg_ref, ang_ref,
                           mixbuf_ref, new_state, pace),
        "conv": _conv_stages(rest_ref, wdw_ref, blg_ref, blb_ref, cbuf_ref, mixbuf_ref, conv_w,
                             pace),
        "gmlp": _gmlp_stages(rest_ref, clg_ref, clb_ref, ws_ref, bs_ref, mixbuf_ref, pace),
    }
    for name in _STEP_ORDER:
        next(stages[name])
    assert all(next(gen, "done") == "done" for gen in stages.values())

    @pl.when(g < n_total)
    def _():
        srun_ref[...] = new_state[0]
        sfin_ref[0] = new_state[0]


def _mixback(x, kv, lr, rest, sbin, s0, mod, mod_row, mixer_w, back_w, final_g,
             *, batch, seq, conv_w, final):
    n_tok = x.shape[0]
    tile = TOK_TILE
    n_tiles = seq // tile
    n_total = batch * n_tiles

    def mix_tile(g):
        return jnp.minimum(g, n_total - 1)

    def back_tile(g):
        return jnp.maximum(g - 1, 0)

    weights = (*mixer_w, *back_w, final_g)
    return pl.pallas_call(
        functools.partial(_mixback_kernel, n_total=n_total, n_tiles=n_tiles, conv_w=conv_w,
                          final=final),
        grid=(n_total + 1,),
        in_specs=[
            pl.BlockSpec((tile, KV_W), lambda g: (mix_tile(g), 0)),
            pl.BlockSpec((tile, LR_PAD), lambda g: (mix_tile(g), 0)),
            pl.BlockSpec((tile, REST_W), lambda g: (mix_tile(g), 0)),
            pl.BlockSpec((N_CHUNKS, A_KW, A_DV), lambda g: (mix_tile(g), 0, 0)),
            pl.BlockSpec((1, 1, A_KW, A_DV), lambda g: (mix_tile(g) // n_tiles, 0, 0, 0)),
            pl.BlockSpec((tile, D_MODEL), lambda g: (back_tile(g), 0)),
            pl.BlockSpec((1, N_MOD, D_MODEL), lambda g: (mod_row(back_tile(g)), 0, 0)),
        ] + [_const_spec(w.shape) for w in weights],
        out_specs=[
            pl.BlockSpec((tile, D_MODEL), lambda g: (back_tile(g), 0)),
            pl.BlockSpec((1, A_KW, A_DV), lambda g: (mix_tile(g) // n_tiles, 0, 0)),
        ],
        out_shape=[
            jax.ShapeDtypeStruct((n_tok, D_MODEL), F32),
            jax.ShapeDtypeStruct((batch, A_KW, A_DV), F32),
        ],
        scratch_shapes=[
            pltpu.VMEM((tile, D_MODEL), BF16),
            pltpu.VMEM((A_KW, A_DV), F32),
            pltpu.VMEM((tile // conv_w, conv_w + 2 * B_PAD, B_W), F32),
        ],
        compiler_params=pltpu.CompilerParams(
            dimension_semantics=("arbitrary",), vmem_limit_bytes=VMEM_LIMIT),
        name="mixers_outproj_ffn",
    )(kv, lr, rest, sbin, s0, x, mod, *weights)


def _prep_layer(l, norm1_g, w_in, w_a_gate, b_a_gate, a_norm_g, w_dw, b_ln_g, b_ln_b,
                c_ln_g, c_ln_b, w_s, b_s, w_out, norm2_g, w_ff1, w_ff2):
    w = w_in[l]
    q0, k0, v0, g0 = 0, A_KW, 2 * A_KW, 2 * A_KW + A_W
    lr0 = g0 + A_W
    glu0 = lr0 + 2 * A_RANK
    uv0 = glu0 + 2 * B_W
    lr_cols = jnp.pad(w[:, lr0:glu0], ((0, 0), (0, LR_PAD - 2 * A_RANK)))
    w_in_p = jnp.concatenate(
        [w[:, k0:v0], w[:, v0:g0], lr_cols, w[:, q0:k0], w[:, g0:lr0], w[:, glu0:uv0],
         w[:, uv0:uv0 + 2 * C_W]], axis=1).astype(BF16)
    w_gate = jnp.zeros((LR_PAD, 2 * A_KW), F32)
    w_gate = w_gate.at[0:A_RANK, 0:A_KW].set(w_a_gate[l, 0])
    w_gate = w_gate.at[A_RANK:2 * A_RANK, A_KW:2 * A_KW].set(w_a_gate[l, 1]).astype(BF16)
    mixer_w = (
        w_gate,
        b_a_gate[l].reshape(1, 2 * A_KW),
        a_norm_g[l].reshape(1, A_DV),
        jnp.pad(w_dw[l].reshape(B_KSIZE, B_W), ((0, 1), (0, 0))),
        b_ln_g[l].reshape(1, B_W), b_ln_b[l].reshape(1, B_W),
        c_ln_g[l].reshape(1, C_W), c_ln_b[l].reshape(1, C_W),
        w_s[l].astype(BF16),
        jnp.repeat(b_s[l].T, C_HD, axis=1),
    )
    back_w = (w_out[l].astype(BF16), norm2_g[l].reshape(1, D_MODEL),
              w_ff1[l].astype(BF16), w_ff2[l].astype(BF16))
    return norm1_g[l].reshape(1, D_MODEL), w_in_p, mixer_w, back_w


def _layer(x, mod, mod_row, s0, front_w, mixer_w, back_w, final_g, *, batch, seq, conv_w, final):
    norm1_g, w_in_p = front_w
    kv, lr, rest = _front(x, mod, mod_row, norm1_g, w_in_p)
    sbin, s_bwd = _bwd_states(kv, lr, s0, mixer_w[0], mixer_w[1], batch=batch, seq=seq)
    x, s_fwd = _mixback(x, kv, lr, rest, sbin, s0, mod, mod_row, mixer_w, back_w, final_g,
                        batch=batch, seq=seq, conv_w=conv_w, final=final)
    return x, jnp.stack([s_fwd, s_bwd], axis=1)


def kernel(x_prompt, x_sample, c, state_gla, c_ctx, w_mod, b_mod, norm1_g, w_in, w_a_gate,
           b_a_gate, a_norm_g, w_dw, b_ln_g, b_ln_b, c_ln_g, c_ln_b, w_s, b_s, w_out, norm2_g,
           w_ff1, w_ff2, final_g):
    n_layers = w_in.shape[0]
    n_ctx, ctx_len, _ = x_prompt.shape
    n_lat, lat_len, _ = x_sample.shape
    assert ctx_len == TOK_TILE and lat_len % TOK_TILE == 0

    cond_rows = 16
    ctx_row = n_lat
    cond = jnp.zeros((cond_rows, D_MODEL), F32).at[0:n_lat].set(c).at[ctx_row].set(c_ctx)
    mod = _modulation(cond, w_mod, b_mod).reshape(n_layers, cond_rows, N_MOD, D_MODEL)

    lat_tiles = lat_len // TOK_TILE
    y_p = x_prompt.reshape(n_ctx * ctx_len, D_MODEL)
    y_s = x_sample.reshape(n_lat * lat_len, D_MODEL)
    s_zero = jnp.zeros((n_ctx, 2, A_KW, A_DV), F32)
    fg = final_g.reshape(1, D_MODEL)
    ctx_states = []
    for l in range(n_layers):
        norm1, w_in_p, mixer_w, back_w = _prep_layer(
            l, norm1_g, w_in, w_a_gate, b_a_gate, a_norm_g, w_dw, b_ln_g, b_ln_b,
            c_ln_g, c_ln_b, w_s, b_s, w_out, norm2_g, w_ff1, w_ff2)
        final = l == n_layers - 1
        y_p, s_ctx = _layer(y_p, mod[l], lambda i: ctx_row, s_zero, (norm1, w_in_p), mixer_w,
                            back_w, fg, batch=n_ctx, seq=ctx_len, conv_w=ctx_len, final=final)
        ctx_states.append(s_ctx.reshape(n_ctx, 2, A_HEADS, A_DK, A_DV))
        s_lat = state_gla[:, l].astype(F32).reshape(n_lat, 2, A_KW, A_DV)
        y_s, _ = _layer(y_s, mod[l], lambda i: i // lat_tiles, s_lat, (norm1, w_in_p), mixer_w,
                        back_w, fg, batch=n_lat, seq=lat_len, conv_w=GRID_W, final=final)
    new_state = jnp.stack(ctx_states, axis=1).astype(state_gla.dtype)
    return (y_p.reshape(x_prompt.shape), y_s.reshape(x_sample.shape), new_state)
```

```python
import functools

import jax
import jax.numpy as jnp
from jax import lax
from jax.experimental import pallas as pl
from jax.experimental.pallas import tpu as pltpu

F32 = jnp.float32
BF16 = jnp.bfloat16

SUBLANES = 8
D_MODEL = 1024
EPS = 1e-6
N_MOD = 6
A_HEADS = 4
A_DK = 64
A_DV = 128
A_KW = A_HEADS * A_DK
A_W = A_HEADS * A_DV
A_RANK = 16
A_TAU = 16.0
A_CHUNK = 64
B_W = 256
B_KSIZE = 31
B_PAD = 16
GRID_W = 64
C_W = 256
C_HEADS = 4
C_HD = 64
C_CHUNK = 128
D_FF = 4096
LR_PAD = 128

KV_W = A_KW + A_W
REST_W = A_KW + A_W + 2 * B_W + 2 * C_W
IN_W = KV_W + LR_PAD + REST_W

TOK_TILE = 256
N_CHUNKS = TOK_TILE // A_CHUNK
FF_CHUNK = 1024
VMEM_LIMIT = 56 * 1024 * 1024


def _dot(a, b):
    return jnp.dot(a, b, preferred_element_type=F32)


def _const_spec(shape):
    zeros = (0,) * len(shape)
    return pl.BlockSpec(shape, lambda *_: zeros, pipeline_mode=pl.Buffered(1))


def _mod_kernel(cond_ref, w_ref, b_ref, out_ref):
    c = cond_ref[...]
    s = c * jax.nn.sigmoid(c)
    out_ref[0] = _dot(s.astype(BF16), w_ref[0].astype(BF16)) + b_ref[0]


def _modulation(cond, w_mod, b_mod):
    n_layers = w_mod.shape[0]
    rows = cond.shape[0]
    return pl.pallas_call(
        _mod_kernel,
        grid=(n_layers, N_MOD),
        in_specs=[
            pl.BlockSpec((rows, D_MODEL), lambda l, j: (0, 0)),
            pl.BlockSpec((1, D_MODEL, D_MODEL), lambda l, j: (l, 0, j)),
            pl.BlockSpec((1, 1, D_MODEL), lambda l, j: (l, 0, j)),
        ],
        out_specs=pl.BlockSpec((1, rows, D_MODEL), lambda l, j: (l, 0, j)),
        out_shape=jax.ShapeDtypeStruct((n_layers, rows, N_MOD * D_MODEL), F32),
        name="adaln_modulation",
    )(cond, w_mod, b_mod.reshape(n_layers, 1, N_MOD * D_MODEL))


def _rms(x, g):
    ms = jnp.mean(x * x, axis=-1, keepdims=True)
    return x * lax.rsqrt(ms + EPS) * g


def _log_sigmoid(x):
    return jnp.minimum(x, 0.0) - jnp.log(1.0 + jnp.exp(-jnp.abs(x)))


def _gate_logs(lr_ref, wg_ref, bg_ref, lo, hi):
    logit = _dot(lr_ref[...].astype(BF16), wg_ref[:, lo:hi]) + bg_ref[:, lo:hi]
    return _log_sigmoid(logit) * (1.0 / A_TAU)


def _chunk_tri(upper, zero=0.0):
    row = lax.broadcasted_iota(jnp.int32, (TOK_TILE, TOK_TILE), 0)
    col = lax.broadcasted_iota(jnp.int32, (TOK_TILE, TOK_TILE), 1)
    same_chunk = (row // A_CHUNK) == (col // A_CHUNK)
    tri = (col >= row) if upper else (row >= col)
    return jnp.where(same_chunk & tri, 1.0 + zero, zero).astype(BF16)


def _split3(x):
    hi = x.astype(BF16)
    r1 = x - hi.astype(F32)
    mid = r1.astype(BF16)
    lo = (r1 - mid.astype(F32)).astype(BF16)
    return hi, mid, lo


def _tri_apply(tri, parts):
    hi, mid, lo = parts
    return _dot(tri, hi) + _dot(tri, mid) + _dot(tri, lo)


def _tri_cumsum(tri, x):
    return _tri_apply(tri, _split3(x))


def _pace_token(x):
    i = x[-SUBLANES:, -128:].astype(jnp.int32)
    z = lax.shift_right_logical(lax.shift_right_logical(i, 16), 16)
    return z[0:1, :].astype(F32)


def _sum_tokens(toks):
    total = toks[0]
    for tok in toks[1:]:
        total = total + tok
    return total


def _lanes(tok, width):
    return jnp.concatenate([tok] * (width // tok.shape[1]), axis=1)


def _pace_zero(pace, width):
    if len(pace["back"]) < 2:
        return jnp.zeros((1, width), F32)
    return _lanes(pace["back"][-2], width)


def _head_of_lane():
    return lax.broadcasted_iota(jnp.int32, (1, A_KW), 1) // A_DK


def _head_stack(x, head_of_lane):
    zero = jnp.zeros_like(x)
    return jnp.concatenate(
        [jnp.where(head_of_lane == h, x, zero) for h in range(A_HEADS)], axis=0)


def _v_stack(v):
    return jnp.concatenate(
        [v[:, h * A_DV:(h + 1) * A_DV] for h in range(A_HEADS)], axis=0)


def _col_bcast(row):
    return jnp.broadcast_to(row, (A_DV, A_KW)).T


def _chunk_increments(k_end, v, head_of_lane):
    ks = k_end.astype(BF16)
    vs = v.astype(BF16)
    tn = (((0,), (0,)), ((), ()))
    out = []
    for c in range(N_CHUNKS):
        rs = slice(c * A_CHUNK, (c + 1) * A_CHUNK)
        out.append(lax.dot_general(_head_stack(ks[rs], head_of_lane), _v_stack(vs[rs]), tn,
                                   preferred_element_type=F32))
    return out


def _per_chunk_rows(x, offset):
    return jnp.concatenate(
        [jnp.broadcast_to(x[c * A_CHUNK + offset:c * A_CHUNK + offset + 1], (A_CHUNK, x.shape[1]))
         for c in range(N_CHUNKS)], axis=0)


def _layernorm(x, g, b):
    mu = jnp.mean(x, axis=-1, keepdims=True)
    xc = x - mu
    var = jnp.mean(xc * xc, axis=-1, keepdims=True)
    return xc * lax.rsqrt(var + EPS) * g + b


def _front_kernel(x_ref, mod_ref, s0_ref, g_ref, w_ref, wg_ref, bg_ref,
                  kv_ref, lr_ref, rest_ref, sbin_ref, sfin_ref,
                  kvbuf_ref, lrbuf_ref, srun_ref, *, n_tiles):
    s = pl.program_id(0)
    L = A_CHUNK

    @pl.when(s == 0)
    def _():
        kvbuf_ref[...] = jnp.zeros(kvbuf_ref.shape, F32)
        lrbuf_ref[...] = jnp.zeros(lrbuf_ref.shape, F32)

    @pl.when(jnp.maximum(s - 1, 0) % n_tiles == 0)
    def _():
        srun_ref[...] = s0_ref[0, 0]

    la = _gate_logs(lrbuf_ref, wg_ref, bg_ref, A_KW, 2 * A_KW)
    m = mod_ref[0]
    hb = (_rms(x_ref[...], g_ref[...]) * (1.0 + m[1:2]) + m[0:1]).astype(BF16)
    kv_ref[...] = _dot(hb, w_ref[:, 0:KV_W])
    b_suf = _tri_cumsum(_chunk_tri(True), la)
    rest0 = KV_W + LR_PAD
    split = REST_W // 2 // 128 * 128
    rest_ref[:, 0:split] = _dot(hb, w_ref[:, rest0:rest0 + split])
    k = kvbuf_ref[:, 0:A_KW]
    v = kvbuf_ref[:, A_KW:KV_W]
    tot = _per_chunk_rows(b_suf, 0)
    ds = _chunk_increments(k * jnp.exp(tot - b_suf), v, _head_of_lane())
    rest_ref[:, split:REST_W] = _dot(hb, w_ref[:, rest0 + split:IN_W])
    lr_ref[...] = _dot(hb, w_ref[:, KV_W:KV_W + LR_PAD])
    state = srun_ref[...]
    for c in reversed(range(N_CHUNKS)):
        sbin_ref[c] = state
        state = _col_bcast(jnp.exp(b_suf[c * L:c * L + 1])) * state + ds[c]
    kvbuf_ref[...] = kv_ref[...]
    lrbuf_ref[...] = lr_ref[...]

    @pl.when(s > 0)
    def _():
        srun_ref[...] = state
        sfin_ref[0] = state


def _front(x, mod, mod_row, s0, norm_g, w_in, w_gate, b_gate, *, batch, seq):
    n_tok = x.shape[0]
    tile = TOK_TILE
    n_tiles = seq // tile
    n_total = batch * n_tiles

    def tile_of(step):
        return (step // n_tiles) * n_tiles + (n_tiles - 1 - step % n_tiles)

    def proj_tile(s):
        return tile_of(jnp.minimum(s, n_total - 1))

    def sweep_step(s):
        return jnp.maximum(s - 1, 0)

    return pl.pallas_call(
        functools.partial(_front_kernel, n_tiles=n_tiles),
        grid=(n_total + 1,),
        in_specs=[
            pl.BlockSpec((tile, D_MODEL), lambda s: (proj_tile(s), 0)),
            pl.BlockSpec((1, N_MOD, D_MODEL), lambda s: (mod_row(proj_tile(s)), 0, 0)),
            pl.BlockSpec((1, 1, A_KW, A_DV), lambda s: (sweep_step(s) // n_tiles, 1, 0, 0)),
            _const_spec((1, D_MODEL)),
            _const_spec((D_MODEL, IN_W)),
            _const_spec(w_gate.shape), _const_spec(b_gate.shape),
        ],
        out_specs=[
            pl.BlockSpec((tile, KV_W), lambda s: (proj_tile(s), 0)),
            pl.BlockSpec((tile, LR_PAD), lambda s: (proj_tile(s), 0)),
            pl.BlockSpec((tile, REST_W), lambda s: (proj_tile(s), 0)),
            pl.BlockSpec((N_CHUNKS, A_KW, A_DV), lambda s: (tile_of(sweep_step(s)), 0, 0)),
            pl.BlockSpec((1, A_KW, A_DV), lambda s: (sweep_step(s) // n_tiles, 0, 0)),
        ],
        out_shape=[
            jax.ShapeDtypeStruct((n_tok, KV_W), F32),
            jax.ShapeDtypeStruct((n_tok, LR_PAD), F32),
            jax.ShapeDtypeStruct((n_tok, REST_W), F32),
            jax.ShapeDtypeStruct((n_tok // A_CHUNK, A_KW, A_DV), F32),
            jax.ShapeDtypeStruct((batch, A_KW, A_DV), F32),
        ],
        scratch_shapes=[
            pltpu.VMEM((tile, KV_W), F32),
            pltpu.VMEM((tile, LR_PAD), F32),
            pltpu.VMEM((A_KW, A_DV), F32),
        ],
        compiler_params=pltpu.CompilerParams(
            dimension_semantics=("arbitrary",), vmem_limit_bytes=VMEM_LIMIT),
        name="front_inproj_bwdstates",
    )(x, mod, s0, norm_g, w_in, w_gate, b_gate)


def _back_stages(x_ref, mixbuf_ref, mod_ref, wout_ref, g2_ref, w1_ref, w2_ref, fg_ref, out_ref,
                 final, pace):
    m = mod_ref[0]
    x1 = x_ref[...] + m[2:3] * _dot(mixbuf_ref[...], wout_ref[...])
    h2 = _rms(x1, g2_ref[...]) * (1.0 + m[4:5]) + m[3:4]
    hb = h2.astype(BF16)
    acc = jnp.zeros(x1.shape, F32)
    pace["back"].append(_pace_token(h2))
    yield
    for c in range(D_FF // FF_CHUNK):
        cs = slice(c * FF_CHUNK, (c + 1) * FF_CHUNK)
        floor = 0.0
        if pace["mix"]:
            floor = _lanes(_sum_tokens(pace["mix"]), FF_CHUNK)
            pace["mix"].clear()
        a = jnp.maximum(_dot(hb, w1_ref[:, cs]), floor)
        acc = acc + _dot((a * a).astype(BF16), w2_ref[cs, :])
        pace["back"].append(_pace_token(acc))
        yield
    gate2 = m[5:6]
    if pace["mix"]:
        gate2 = gate2 + _lanes(_sum_tokens(pace["mix"]), D_MODEL)
    x2 = x1 + gate2 * acc
    if final:
        x2 = _rms(x2, fg_ref[...])
    out_ref[...] = x2
    yield


def _gla_stages(kv_ref, lr_ref, rest_ref, sbin_ref, srun_ref, wg_ref, bg_ref, ang_ref, mix_ref,
                state_out, pace):
    L = A_CHUNK
    head_of_lane = _head_of_lane()
    la = _gate_logs(lr_ref, wg_ref, bg_ref, 0, 2 * A_KW)
    parts_f = _split3(la[:, 0:A_KW])
    parts_b = _split3(la[:, A_KW:2 * A_KW])
    yield

    zero = _pace_zero(pace, TOK_TILE)
    k = kv_ref[:, 0:A_KW]
    v = kv_ref[:, A_KW:KV_W]
    q = rest_ref[:, 0:A_KW] * (A_DK ** -0.5)
    b_pre = _tri_apply(_chunk_tri(False, zero), parts_f)
    b_suf = _tri_apply(_chunk_tri(True, zero), parts_b)
    tot_f = _per_chunk_rows(b_pre, L - 1)

    qf = (q * jnp.exp(b_pre)).astype(BF16)
    kf = (k * jnp.exp(-b_pre)).astype(BF16)
    qb = (q * jnp.exp(b_suf)).astype(BF16)
    kb = (k * jnp.exp(-b_suf)).astype(BF16)
    ds = _chunk_increments(k * jnp.exp(tot_f - b_pre), v, head_of_lane)
    vb = v.astype(BF16)
    pace["mix"].append(_pace_token(ds[-1]))
    yield

    row_c = lax.broadcasted_iota(jnp.int32, (A_HEADS * L, L), 0) % L
    col_c = lax.broadcasted_iota(jnp.int32, (A_HEADS * L, L), 1)
    nt = (((1,), (1,)), ((), ()))
    ang = ang_ref[...]
    state = srun_ref[...]
    for c in range(N_CHUNKS):
        rs = slice(c * L, (c + 1) * L)
        zero = _pace_zero(pace, A_KW).astype(BF16)
        qf4 = _head_stack(qf[rs], head_of_lane)
        qb4 = _head_stack(qb[rs], head_of_lane)
        att_f = lax.dot_general(qf4, kf[rs] + zero, nt, preferred_element_type=F32)
        att_b = lax.dot_general(qb4, kb[rs] + zero, nt, preferred_element_type=F32)
        att = (jnp.where(row_c >= col_c, att_f, 0.0)
               + jnp.where(col_c >= row_c, att_b, 0.0)).astype(BF16)
        s_cat = jnp.concatenate([state, sbin_ref[c]], axis=0).astype(BF16)
        o = _dot(jnp.concatenate([qf4, qb4], axis=1), s_cat)
        for h in range(A_HEADS):
            hs = slice(h * L, (h + 1) * L)
            vs = slice(h * A_DV, (h + 1) * A_DV)
            oh = _rms(o[hs] + _dot(att[hs], vb[rs, vs]), ang)
            gh = rest_ref[rs, A_KW + h * A_DV:A_KW + (h + 1) * A_DV]
            oh = oh * (gh * jax.nn.sigmoid(gh))
            mix_ref[rs, vs] = oh.astype(BF16)
            pace["mix"].append(_pace_token(oh))
        state = _col_bcast(jnp.exp(b_pre[c * L + L - 1:c * L + L])) * state + ds[c]
        if c == N_CHUNKS - 1:
            state_out.append(state)
        yield


def _conv_stages(rest_ref, wdw_ref, blg_ref, blb_ref, cbuf_ref, mix_ref, conv_w, pace):
    glu_lo = A_KW + A_W
    for r in range(TOK_TILE // conv_w):
        rs = slice(r * conv_w, (r + 1) * conv_w)
        a = rest_ref[rs, glu_lo:glu_lo + B_W]
        b = rest_ref[rs, glu_lo + B_W:glu_lo + 2 * B_W]
        cbuf_ref[r, 0:B_PAD, :] = jnp.zeros((B_PAD, B_W), F32)
        cbuf_ref[r, B_PAD + conv_w:2 * B_PAD + conv_w, :] = jnp.zeros((B_PAD, B_W), F32)
        cbuf_ref[r, B_PAD:B_PAD + conv_w, :] = a * jax.nn.sigmoid(b)
    blk = GRID_W
    assert B_PAD - B_KSIZE // 2 == 1
    ext = blk + SUBLANES
    for i in range(TOK_TILE // blk):
        r, off = divmod(i * blk, conv_w)
        zero = _pace_zero(pace, B_W)
        acc = None
        for res in range(SUBLANES):
            part = None
            for m in range(res if res else SUBLANES, B_KSIZE + 1, SUBLANES):
                lo = off + m - res
                term = (wdw_ref[m - 1:m, :] + zero) * cbuf_ref[r, lo:lo + ext, :]
                part = term if part is None else part + term
            part = part[res:res + blk]
            acc = part if acc is None else acc + part
        y = _layernorm(acc, blg_ref[...], blb_ref[...])
        y = y * jax.nn.sigmoid(y)
        mix_ref[i * blk:(i + 1) * blk, A_W:A_W + B_W] = y.astype(BF16)
        pace["mix"].append(_pace_token(y))
        yield


def _gmlp_stages(rest_ref, clg_ref, clb_ref, ws_ref, bs_ref, mix_ref, pace):
    uv_lo = A_KW + A_W + 2 * B_W
    head_of_lane_c = lax.broadcasted_iota(jnp.int32, (1, C_W), 1) // C_HD
    for n in range(TOK_TILE // C_CHUNK):
        rs = slice(n * C_CHUNK, (n + 1) * C_CHUNK)
        ge = jax.nn.gelu(rest_ref[rs, uv_lo:uv_lo + 2 * C_W] + _pace_zero(pace, 2 * C_W))
        u = ge[:, 0:C_W]
        vn = _layernorm(ge[:, C_W:2 * C_W], clg_ref[...], clb_ref[...]).astype(BF16)
        sv = bs_ref[...]
        for h in range(C_HEADS):
            sv = sv + jnp.where(head_of_lane_c == h, _dot(ws_ref[h], vn), 0.0)
        gated = u * sv
        mix_ref[rs, A_W + B_W:A_W + B_W + C_W] = gated.astype(BF16)
        pace["mix"].append(_pace_token(gated))
        yield


_STEP_ORDER = ("gla", "back",
               "back", "gla", "conv",
               "back", "gla", "gla", "gmlp", "conv",
               "back", "gla", "gla", "gmlp", "conv",
               "back", "conv",
               "back")


def _mixback_kernel(kv_ref, lr_ref, rest_ref, sbin_ref, s0_ref, x_ref, mod_ref,
                    wg_ref, bg_ref, ang_ref, wdw_ref, blg_ref, blb_ref, clg_ref, clb_ref,
                    ws_ref, bs_ref, wout_ref, g2_ref, w1_ref, w2_ref, fg_ref,
                    out_ref, sfin_ref, mixbuf_ref, srun_ref, cbuf_ref,
                    *, n_total, n_tiles, conv_w, final):
    g = pl.program_id(0)

    @pl.when(g == 0)
    def _():
        mixbuf_ref[...] = jnp.zeros(mixbuf_ref.shape, BF16)

    @pl.when(jnp.minimum(g, n_total - 1) % n_tiles == 0)
    def _():
        srun_ref[...] = s0_ref[0, 0]

    new_state = []
    pace = {"back": [], "mix": []}
    stages = {
        "back": _back_stages(x_ref, mixbuf_ref, mod_ref, wout_ref, g2_ref, w1_ref, w2_ref, fg_ref,
                             out_ref, final, pace),
        "gla": _gla_stages(kv_ref, lr_ref, rest_ref, sbin_ref, srun_ref, wg_ref, bg_ref, ang_ref,
                           mixbuf_ref, new_state, pace),
        "conv": _conv_stages(rest_ref, wdw_ref, blg_ref, blb_ref, cbuf_ref, mixbuf_ref, conv_w,
                             pace),
        "gmlp": _gmlp_stages(rest_ref, clg_ref, clb_ref, ws_ref, bs_ref, mixbuf_ref, pace),
    }
    for name in _STEP_ORDER:
        next(stages[name])
    assert all(next(gen, "done") == "done" for gen in stages.values())

    @pl.when(g < n_total)
    def _():
        srun_ref[...] = new_state[0]
        sfin_ref[0] = new_state[0]


def _mixback(x, kv, lr, rest, sbin, s0, mod, mod_row, mixer_w, back_w, final_g,
             *, batch, seq, conv_w, final):
    n_tok = x.shape[0]
    tile = TOK_TILE
    n_tiles = seq // tile
    n_total = batch * n_tiles

    def mix_tile(g):
        return jnp.minimum(g, n_total - 1)

    def back_tile(g):
        return jnp.maximum(g - 1, 0)

    weights = (*mixer_w, *back_w, final_g)
    return pl.pallas_call(
        functools.partial(_mixback_kernel, n_total=n_total, n_tiles=n_tiles, conv_w=conv_w,
                          final=final),
        grid=(n_total + 1,),
        in_specs=[
            pl.BlockSpec((tile, KV_W), lambda g: (mix_tile(g), 0)),
            pl.BlockSpec((tile, LR_PAD), lambda g: (mix_tile(g), 0)),
            pl.BlockSpec((tile, REST_W), lambda g: (mix_tile(g), 0)),
            pl.BlockSpec((N_CHUNKS, A_KW, A_DV), lambda g: (mix_tile(g), 0, 0)),
            pl.BlockSpec((1, 1, A_KW, A_DV), lambda g: (mix_tile(g) // n_tiles, 0, 0, 0)),
            pl.BlockSpec((tile, D_MODEL), lambda g: (back_tile(g), 0)),
            pl.BlockSpec((1, N_MOD, D_MODEL), lambda g: (mod_row(back_tile(g)), 0, 0)),
        ] + [_const_spec(w.shape) for w in weights],
        out_specs=[
            pl.BlockSpec((tile, D_MODEL), lambda g: (back_tile(g), 0)),
            pl.BlockSpec((1, A_KW, A_DV), lambda g: (mix_tile(g) // n_tiles, 0, 0)),
        ],
        out_shape=[
            jax.ShapeDtypeStruct((n_tok, D_MODEL), F32),
            jax.ShapeDtypeStruct((batch, A_KW, A_DV), F32),
        ],
        scratch_shapes=[
            pltpu.VMEM((tile, D_MODEL), BF16),
            pltpu.VMEM((A_KW, A_DV), F32),
            pltpu.VMEM((tile // conv_w, conv_w + 2 * B_PAD, B_W), F32),
        ],
        compiler_params=pltpu.CompilerParams(
            dimension_semantics=("arbitrary",), vmem_limit_bytes=VMEM_LIMIT),
        name="mixers_outproj_ffn",
    )(kv, lr, rest, sbin, s0, x, mod, *weights)


def _prep_layer(l, norm1_g, w_in, w_a_gate, b_a_gate, a_norm_g, w_dw, b_ln_g, b_ln_b,
                c_ln_g, c_ln_b, w_s, b_s, w_out, norm2_g, w_ff1, w_ff2):
    w = w_in[l]
    q0, k0, v0, g0 = 0, A_KW, 2 * A_KW, 2 * A_KW + A_W
    lr0 = g0 + A_W
    glu0 = lr0 + 2 * A_RANK
    uv0 = glu0 + 2 * B_W
    lr_cols = jnp.pad(w[:, lr0:glu0], ((0, 0), (0, LR_PAD - 2 * A_RANK)))
    w_in_p = jnp.concatenate(
        [w[:, k0:v0], w[:, v0:g0], lr_cols, w[:, q0:k0], w[:, g0:lr0], w[:, glu0:uv0],
         w[:, uv0:uv0 + 2 * C_W]], axis=1).astype(BF16)
    w_gate = jnp.zeros((LR_PAD, 2 * A_KW), F32)
    w_gate = w_gate.at[0:A_RANK, 0:A_KW].set(w_a_gate[l, 0])
    w_gate = w_gate.at[A_RANK:2 * A_RANK, A_KW:2 * A_KW].set(w_a_gate[l, 1]).astype(BF16)
    mixer_w = (
        w_gate,
        b_a_gate[l].reshape(1, 2 * A_KW),
        a_norm_g[l].reshape(1, A_DV),
        jnp.pad(w_dw[l].reshape(B_KSIZE, B_W), ((0, 1), (0, 0))),
        b_ln_g[l].reshape(1, B_W), b_ln_b[l].reshape(1, B_W),
        c_ln_g[l].reshape(1, C_W), c_ln_b[l].reshape(1, C_W),
        w_s[l].astype(BF16),
        jnp.repeat(b_s[l].T, C_HD, axis=1),
    )
    back_w = (w_out[l].astype(BF16), norm2_g[l].reshape(1, D_MODEL),
              w_ff1[l].astype(BF16), w_ff2[l].astype(BF16))
    return norm1_g[l].reshape(1, D_MODEL), w_in_p, mixer_w, back_w


def _layer(x, mod, mod_row, s0, front_w, mixer_w, back_w, final_g, *, batch, seq, conv_w, final):
    norm1_g, w_in_p = front_w
    kv, lr, rest, sbin, s_bwd = _front(x, mod, mod_row, s0, norm1_g, w_in_p, mixer_w[0],
                                       mixer_w[1], batch=batch, seq=seq)
    x, s_fwd = _mixback(x, kv, lr, rest, sbin, s0, mod, mod_row, mixer_w, back_w, final_g,
                        batch=batch, seq=seq, conv_w=conv_w, final=final)
    return x, jnp.stack([s_fwd, s_bwd], axis=1)


def kernel(x_prompt, x_sample, c, state_gla, c_ctx, w_mod, b_mod, norm1_g, w_in, w_a_gate,
           b_a_gate, a_norm_g, w_dw, b_ln_g, b_ln_b, c_ln_g, c_ln_b, w_s, b_s, w_out, norm2_g,
           w_ff1, w_ff2, final_g):
    n_layers = w_in.shape[0]
    n_ctx, ctx_len, _ = x_prompt.shape
    n_lat, lat_len, _ = x_sample.shape
    assert ctx_len == TOK_TILE and lat_len % TOK_TILE == 0

    cond_rows = 16
    ctx_row = n_lat
    cond = jnp.zeros((cond_rows, D_MODEL), F32).at[0:n_lat].set(c).at[ctx_row].set(c_ctx)
    mod = _modulation(cond, w_mod, b_mod).reshape(n_layers, cond_rows, N_MOD, D_MODEL)

    lat_tiles = lat_len // TOK_TILE
    y_p = x_prompt.reshape(n_ctx * ctx_len, D_MODEL)
    y_s = x_sample.reshape(n_lat * lat_len, D_MODEL)
    s_zero = jnp.zeros((n_ctx, 2, A_KW, A_DV), F32)
    fg = final_g.reshape(1, D_MODEL)
    ctx_states = []
    for l in range(n_layers):
        norm1, w_in_p, mixer_w, back_w = _prep_layer(
            l, norm1_g, w_in, w_a_gate, b_a_gate, a_norm_g, w_dw, b_ln_g, b_ln_b,
            c_ln_g, c_ln_b, w_s, b_s, w_out, norm2_g, w_ff1, w_ff2)
        final = l == n_layers - 1
        y_p, s_ctx = _layer(y_p, mod[l], lambda i: ctx_row, s_zero, (norm1, w_in_p), mixer_w,
                            back_w, fg, batch=n_ctx, seq=ctx_len, conv_w=ctx_len, final=final)
        ctx_states.append(s_ctx.reshape(n_ctx, 2, A_HEADS, A_DK, A_DV))
        s_lat = state_gla[:, l].astype(F32).reshape(n_lat, 2, A_KW, A_DV)
        y_s, _ = _layer(y_s, mod[l], lambda i: i // lat_tiles, s_lat, (norm1, w_in_p), mixer_w,
                        back_w, fg, batch=n_lat, seq=lat_len, conv_w=GRID_W, final=final)
    new_state = jnp.stack(ctx_states, axis=1).astype(state_gla.dtype)
    return (y_p.reshape(x_prompt.shape), y_s.reshape(x_sample.shape), new_state)
```
